```python
import jax, jax.numpy as jnp
from jax import lax
import numpy as np

D_MODEL = 1024
BATCH = 2
SEQ = 8192
DEPTH = 4
DEC_BATCH = 128
DEC_SEQ = 4
PAST_LEN = 2048
PAGE_SIZE = 128

HEAD_DIM = 64
ATT_W = D_MODEL // 2
N_HEADS = ATT_W // HEAD_DIM
GM_GROUP_DIM = 64
GM_W = D_MODEL // 4
GM_GROUPS = GM_W // GM_GROUP_DIM
CONV_W = D_MODEL // 4
CONV_K = 3
CHUNK = 128
SB_BLOCK = 128
D_FF = 4 * D_MODEL
N_BRANCHES = 3
N_IN = 3 * ATT_W + 2 * GM_W + 3 * CONV_W + N_BRANCHES * D_MODEL
POOL_NUM = 5
POOL_DEN = 4
SB_BIAS_INIT = -6.0
EPS = 1e-6

kernel_name = "hybrid_stickbreak_gmlp_shortconv_decode_step"


def _rmsnorm(x, g):
    xf = x.astype(jnp.float32)
    y = xf * lax.rsqrt(jnp.mean(xf * xf, axis=-1, keepdims=True) + EPS)
    return (y * g.astype(jnp.float32)).astype(x.dtype)


def _split_in(proj):
    widths = [ATT_W] * 3 + [GM_W] * 2 + [CONV_W] * 3 + [N_BRANCHES * D_MODEL]
    idx = [int(i) for i in np.cumsum(widths)[:-1]]
    return jnp.split(proj, idx, axis=-1)


def _sb_attend(q, q_pos, k, v, k_pos, bias):
    z = (jnp.einsum('nqhd,nkhd->nhqk', q, k).astype(jnp.float32) * (HEAD_DIM ** -0.5)
         + bias.astype(jnp.float32)[None, :, None, None])
    mask = (k_pos[None, :] < q_pos[:, None])[None, None]
    log_beta = jax.nn.log_sigmoid(z)
    log_rest = jnp.where(mask, jax.nn.log_sigmoid(-z), 0.0)
    after = lax.cumsum(log_rest, axis=3, reverse=True) - log_rest
    a = jnp.where(mask, jnp.exp(log_beta + after), 0.0)
    return jnp.einsum('nhqk,nkhd->nqhd', a.astype(v.dtype), v)


def _stick_breaking_prompt(q, k, v, bias):
    n, s, h, d = q.shape
    nb = s // SB_BLOCK
    qb = q.reshape(n, nb, SB_BLOCK, h, d).swapaxes(0, 1)
    pos = jnp.arange(s, dtype=jnp.int32)
    qpos = pos.reshape(nb, SB_BLOCK)
    ob = lax.map(lambda a: _sb_attend(a[0], a[1], k, v, pos, bias), (qb, qpos))
    return ob.swapaxes(0, 1).reshape(n, s, h, d)


def _chunk_spatial(v, w_s, b_s):
    n, s = v.shape[0], v.shape[1]
    pad = (-s) % CHUNK
    vp = jnp.pad(v, ((0, 0), (0, pad), (0, 0), (0, 0)))
    nc = (s + pad) // CHUNK
    vc = vp.reshape(n, nc, CHUNK, GM_GROUPS, GM_GROUP_DIM)
    w = w_s * jnp.tril(jnp.ones((CHUNK, CHUNK), w_s.dtype))
    out = jnp.einsum('gts,ncsgd->nctgd', w, vc) + b_s.T[:, :, None]
    return out.reshape(n, nc * CHUNK, GM_GROUPS, GM_GROUP_DIM)[:, :s]


def _short_conv(x, buf, w):
    s = x.shape[1]
    full = jnp.concatenate([buf, x], axis=1)
    y = full[:, 0:s] * w[0]
    for i in range(1, CONV_K):
        y = y + full[:, i:i + s] * w[i]
    return y, full[:, -(CONV_K - 1):]


def _layer(x, past_k, past_v, conv_buf, norm_mix, w_in, q_gain, k_gain, sb_bias, gm_gain,
           w_spatial, b_spatial, conv_w, w_proj_attn, w_proj_gmlp, w_proj_conv,
           w_out, norm_mlp, w_up, w_down):
    n, s, _ = x.shape
    xn = _rmsnorm(x, norm_mix)
    q, k, v, u, gv, cb, cc, cx, gates = _split_in(xn @ w_in)
    q = _rmsnorm(q.reshape(n, s, N_HEADS, HEAD_DIM), q_gain)
    k = _rmsnorm(k.reshape(n, s, N_HEADS, HEAD_DIM), k_gain)
    v = v.reshape(n, s, N_HEADS, HEAD_DIM)
    if past_k is None:
        att = _stick_breaking_prompt(q, k, v, sb_bias)
    else:
        p = past_k.shape[1]
        k_all = jnp.concatenate([past_k, k], axis=1)
        v_all = jnp.concatenate([past_v, v], axis=1)
        att = _sb_attend(q, p + jnp.arange(s, dtype=jnp.int32), k_all, v_all,
                         jnp.arange(p + s, dtype=jnp.int32), sb_bias)
    branch_a = att.reshape(n, s, ATT_W) @ w_proj_attn
    gvn = _rmsnorm(gv.reshape(n, s, GM_GROUPS, GM_GROUP_DIM),
                   gm_gain.reshape(GM_GROUPS, GM_GROUP_DIM))
    mixed = _chunk_spatial(gvn, w_spatial, b_spatial).reshape(n, s, GM_W)
    branch_b = (u * mixed) @ w_proj_gmlp
    if conv_buf is None:
        conv_buf = jnp.zeros((n, CONV_K - 1, CONV_W), x.dtype)
    cy, new_buf = _short_conv(cc * cx, conv_buf, conv_w)
    branch_c = (cb * cy) @ w_proj_conv
    g_a, g_b, g_c = jnp.split(gates, N_BRANCHES, axis=-1)
    merged = (jax.nn.sigmoid(g_a) * branch_a + jax.nn.sigmoid(g_b) * branch_b
              + jax.nn.sigmoid(g_c) * branch_c)
    x = x + merged @ w_out
    h = jnp.square(jax.nn.relu(_rmsnorm(x, norm_mlp) @ w_up))
    x = x + h @ w_down
    return x, k, v, new_buf, gvn.reshape(n, s, GM_W)


def setup_inputs(seed: int = 0) -> dict:
    key = jax.random.key(seed)
    ks = jax.random.split(key, 24)
    n_pages = PAST_LEN // PAGE_SIZE
    n_pool = DEC_BATCH * n_pages * POOL_NUM // POOL_DEN

    def nrm(k, shape, scale):
        return jax.random.normal(k, shape, jnp.float32) * scale

    page_table = jax.random.permutation(ks[5], n_pool)[:DEC_BATCH * n_pages]
    page_table = page_table.reshape(DEC_BATCH, n_pages).astype(jnp.int32)
    return {
        "x_prompt": nrm(ks[0], (BATCH, SEQ, D_MODEL), 1.0),
        "x_sample": nrm(ks[1], (DEC_BATCH, DEC_SEQ, D_MODEL), 1.0),
        "cache_k": nrm(ks[2], (DEPTH, n_pool, PAGE_SIZE, N_HEADS, HEAD_DIM), 1.0),
        "cache_v": nrm(ks[3], (DEPTH, n_pool, PAGE_SIZE, N_HEADS, HEAD_DIM), 1.0),
        "state_conv": nrm(ks[4], (DEPTH, DEC_BATCH, CONV_K - 1, CONV_W), 1.0),
        "page_table": page_table,
        "norm_mix": 1.0 + nrm(ks[6], (DEPTH, D_MODEL), 0.02),
        "w_in": nrm(ks[7], (DEPTH, D_MODEL, N_IN), D_MODEL ** -0.5),
        "q_gain": 1.0 + nrm(ks[8], (DEPTH, HEAD_DIM), 0.02),
        "k_gain": 1.0 + nrm(ks[9], (DEPTH, HEAD_DIM), 0.02),
        "sb_bias": SB_BIAS_INIT + nrm(ks[21], (DEPTH, N_HEADS), 0.1),
        "gm_gain": 1.0 + nrm(ks[10], (DEPTH, GM_W), 0.02),
        "w_spatial": nrm(ks[11], (DEPTH, GM_GROUPS, CHUNK, CHUNK), CHUNK ** -0.5),
        "b_spatial": 1.0 + nrm(ks[12], (DEPTH, GM_GROUPS, CHUNK), 0.02),
        "conv_w": nrm(ks[13], (DEPTH, CONV_K, CONV_W), CONV_K ** -0.5),
        "w_proj_attn": nrm(ks[14], (DEPTH, ATT_W, D_MODEL), ATT_W ** -0.5),
        "w_proj_gmlp": nrm(ks[15], (DEPTH, GM_W, D_MODEL), GM_W ** -0.5),
        "w_proj_conv": nrm(ks[16], (DEPTH, CONV_W, D_MODEL), CONV_W ** -0.5),
        "w_out": nrm(ks[17], (DEPTH, D_MODEL, D_MODEL), D_MODEL ** -0.5),
        "norm_mlp": 1.0 + nrm(ks[18], (DEPTH, D_MODEL), 0.02),
        "w_up": nrm(ks[19], (DEPTH, D_MODEL, D_FF), D_MODEL ** -0.5),
        "w_down": nrm(ks[20], (DEPTH, D_FF, D_MODEL), D_FF ** -0.5),
    }


def reference(x_prompt, x_sample, cache_k, cache_v, state_conv, page_table,
              norm_mix, w_in, q_gain, k_gain, sb_bias, gm_gain, w_spatial, b_spatial, conv_w,
              w_proj_attn, w_proj_gmlp, w_proj_conv, w_out, norm_mlp, w_up, w_down):
    n_dec = x_sample.shape[0]
    yp, ys = x_prompt, x_sample
    kp_l, vp_l, cp_l, ks_l, vs_l, cs_l, gs_l = [], [], [], [], [], [], []
    for l in range(DEPTH):
        w = (norm_mix[l], w_in[l], q_gain[l], k_gain[l], sb_bias[l], gm_gain[l], w_spatial[l],
             b_spatial[l], conv_w[l], w_proj_attn[l], w_proj_gmlp[l], w_proj_conv[l],
             w_out[l], norm_mlp[l], w_up[l], w_down[l])
        yp, kp, vp, cp, _ = _layer(yp, None, None, None, *w)
        past_k = cache_k[l][page_table].reshape(n_dec, -1, N_HEADS, HEAD_DIM)
        past_v = cache_v[l][page_table].reshape(n_dec, -1, N_HEADS, HEAD_DIM)
        ys, k_s, v_s, c_s, g_s = _layer(ys, past_k, past_v, state_conv[l], *w)
        kp_l.append(kp); vp_l.append(vp); cp_l.append(cp)
        ks_l.append(k_s); vs_l.append(v_s); cs_l.append(c_s); gs_l.append(g_s)
    k_prompt = jnp.stack(kp_l)
    v_prompt = jnp.stack(vp_l)
    conv_prompt = jnp.stack(cp_l)
    k_sample = jnp.stack(ks_l)
    v_sample = jnp.stack(vs_l)
    conv_sample = jnp.stack(cs_l)
    gmlp_v_sample = jnp.stack(gs_l)
    return (yp, ys, k_prompt, v_prompt, conv_prompt, k_sample, v_sample, conv_sample, gmlp_v_sample)
```

```python
import functools

import jax
import jax.numpy as jnp
from jax import lax
from jax.experimental import pallas as pl
from jax.experimental.pallas import tpu as pltpu

HEAD_DIM = 64
GM_GROUP_DIM = 64
GM_GROUPS = 4
CONV_K = 3
CHUNK = 128
EPS = 1e-6
LOG2E = 1.4426950408889634
LN2 = 0.6931471805599453
LANES = 128
D_FF_CHUNK = 1024
BF16 = jnp.bfloat16
F32 = jnp.float32


def _dot(a, b):
    return jnp.dot(a, b, preferred_element_type=F32)


def _dot_nt(a, b):
    return lax.dot_general(a, b, (((1,), (1,)), ((), ())), preferred_element_type=F32)


def _rms_rows(x, gain):
    ms = jnp.mean(x * x, axis=-1, keepdims=True)
    return x * lax.rsqrt(ms + EPS) * gain


def _group_rms_lanes(y, ones_blockdiag, group):
    sq = y * y
    hi = sq.astype(BF16)
    lo = (sq - hi.astype(F32)).astype(BF16)
    ss = _dot(hi, ones_blockdiag) + _dot(lo, ones_blockdiag)
    return y * lax.rsqrt(ss * (1.0 / group) + EPS)


def _group_rms_rows(yt, group):
    f, n = yt.shape
    y3 = yt.reshape(f // group, group, n)
    ss = jnp.sum(y3 * y3, axis=1, keepdims=True)
    return (y3 * lax.rsqrt(ss * (1.0 / group) + EPS)).reshape(f, n)


def _tile_lanes(a, n):
    return a[:, :n] if n <= a.shape[1] else jnp.concatenate([a] * (n // a.shape[1]), axis=1)


def _in_proj_kernel(*refs, sample_mode, tiles_per_seq):
    if sample_mode:
        (x_ref, nm_ref, wq, wkt, wvt, wu, wgv, wcb, wcc, wcx, hs_att, hs_gm, qg, kgt, gmg, cw,
         buf0_ref, buf1_ref, q_o, kf_o, kb_o, vf_o, vb_o, u_o, gvn_o, xc_o, cbcy_o, prev1_ref, prev2_ref) = refs
    else:
        (x_ref, nm_ref, wq, wkt, wvt, wu, wgv, wcb, wcc, wcx, hs_att, hs_gm, qg, kgt, gmg, cw,
         q_o, kf_o, kb_o, vf_o, vb_o, u_o, gvn_o, xc_o, cbcy_o, tail_ref) = refs

    xn = _rms_rows(x_ref[...], nm_ref[...]).astype(BF16)
    tm = xn.shape[0]
    tks = kb_o.shape[2]

    q = _group_rms_lanes(_dot(xn, wq[...]), hs_att[...], HEAD_DIM) * (qg[...] * (LOG2E * HEAD_DIM ** -0.5))
    q_o[...] = q.astype(BF16)
    kt = _group_rms_rows(_dot_nt(wkt[...], xn), HEAD_DIM) * _tile_lanes(kgt[...], tm)
    vt = _dot_nt(wvt[...], xn)
    kf_o[...] = kt
    vf_o[...] = vt
    for j in range(tm // tks):
        kb_o[j] = kt[:, j * tks:(j + 1) * tks].astype(BF16)
        vb_o[j] = vt[:, j * tks:(j + 1) * tks].astype(BF16)
    u_o[...] = _dot(xn, wu[...])
    gvn_o[...] = _group_rms_lanes(_dot(xn, wgv[...]), hs_gm[...], GM_GROUP_DIM) * gmg[...]

    cb = _dot(xn, wcb[...])
    xc = _dot(xn, wcc[...]) * _dot(xn, wcx[...])
    xc_o[...] = xc
    if sample_mode:
        @pl.when(pl.program_id(0) == 0)
        def _():
            prev2_ref[...] = buf0_ref[...]
            prev1_ref[...] = buf1_ref[...]
        prev1 = prev1_ref[...]
        prev2 = prev2_ref[...]
        prev2_ref[...] = prev1
        prev1_ref[...] = xc
    else:
        @pl.when(pl.program_id(0) % tiles_per_seq == 0)
        def _():
            tail_ref[...] = jnp.zeros_like(tail_ref)
        row = lax.broadcasted_iota(jnp.int32, xc.shape, 0)
        last2 = jnp.broadcast_to(tail_ref[0:1, :], xc.shape)
        last1 = jnp.broadcast_to(tail_ref[1:2, :], xc.shape)
        prev1 = jnp.where(row == 0, last1, pltpu.roll(xc, 1, axis=0))
        prev2 = jnp.where(row == 0, last2, jnp.where(row == 1, last1, pltpu.roll(xc, 2, axis=0)))
        tail_ref[0:2, :] = xc[tm - 2:tm, :]
    cy = prev2 * cw[0:1, :] + prev1 * cw[1:2, :] + xc * cw[2:3, :]
    cbcy_o[...] = (cb * cy).astype(BF16)


def _in_proj(x, w, conv_state, tm, n_groups, tk_store):
    m, d = x.shape
    sample_mode = conv_state is not None
    att_w = w["wq"].shape[1]
    gm_w = w["wu"].shape[1]
    group_len = m // n_groups
    tps = group_len // tm
    tks = min(tk_store, tm)
    full = lambda a: pl.BlockSpec(a.shape, lambda i: (0,) * a.ndim)
    rows = lambda width: pl.BlockSpec((tm, width), lambda i: (i, 0))
    ins = [x, w["norm_mix"], w["wq"], w["wkt"], w["wvt"], w["wu"], w["wgv"], w["wcb"], w["wcc"], w["wcx"],
           w["hs_att"], w["hs_gm"], w["q_gain"], w["k_gain_t"], w["gm_gain"], w["conv_w"]]
    in_specs = [rows(d)] + [full(a) for a in ins[1:]]
    if sample_mode:
        ins += list(conv_state)
        in_specs += [full(conv_state[0]), full(conv_state[1])]
    ft_f32 = jax.ShapeDtypeStruct((n_groups, att_w, group_len), F32)
    ft_bf16 = jax.ShapeDtypeStruct((n_groups, group_len // tks, att_w, tks), BF16)
    ft_f32_spec = pl.BlockSpec((None, att_w, tm), lambda i: (i // tps, 0, i % tps))
    ft_bf16_spec = pl.BlockSpec((None, tm // tks, att_w, tks), lambda i: (i // tps, i % tps, 0, 0))
    out_shape = [jax.ShapeDtypeStruct((m, att_w), BF16),
                 ft_f32, ft_bf16,
                 ft_f32, ft_bf16,
                 jax.ShapeDtypeStruct((m, gm_w), F32),
                 jax.ShapeDtypeStruct((m, gm_w), F32),
                 jax.ShapeDtypeStruct((m, gm_w), F32),
                 jax.ShapeDtypeStruct((m, gm_w), BF16)]
    out_specs = [rows(att_w), ft_f32_spec, ft_bf16_spec, ft_f32_spec, ft_bf16_spec,
                 rows(gm_w), rows(gm_w), rows(gm_w), rows(gm_w)]
    if sample_mode:
        scratch = [pltpu.VMEM((tm, gm_w), F32), pltpu.VMEM((tm, gm_w), F32)]
    else:
        scratch = [pltpu.VMEM((8, gm_w), F32)]
    return pl.pallas_call(
        functools.partial(_in_proj_kernel, sample_mode=sample_mode, tiles_per_seq=tps),
        grid=(m // tm,), in_specs=in_specs, out_specs=out_specs, out_shape=out_shape,
        scratch_shapes=scratch,
        compiler_params=pltpu.CompilerParams(dimension_semantics=("arbitrary",)),
        name="in_proj_sample" if sample_mode else "in_proj_prompt",
    )(*ins)


def _sb_block(z, strict_upper, carry, mask):
    e = jnp.exp2(-jnp.abs(z))
    softplus2 = jnp.log(1.0 + e) * (1.0 / LN2)
    lb = jnp.minimum(z, 0.0) - softplus2
    lr = lb - z
    if mask is not None:
        lr = jnp.where(mask, lr, 0.0)
    after = _dot(lr.astype(BF16), strict_upper) + carry
    a = jnp.exp2(lb + after)
    if mask is not None:
        a = jnp.where(mask, a, 0.0)
    return a.astype(BF16), carry + jnp.sum(lr, axis=1, keepdims=True)


def _strict_upper(n):
    j = lax.broadcasted_iota(jnp.int32, (n, n), 0)
    s = lax.broadcasted_iota(jnp.int32, (n, n), 1)
    return jnp.where(j > s, 1.0, 0.0).astype(BF16)


def _sb_prompt_kernel(bias_ref, q_ref, k_ref, v_ref, o_ref, acc_ref, carry_ref, *, tq, tk):
    hp = pl.program_id(1)
    qb = pl.program_id(2)
    ratio = tq // tk
    q2 = q_ref[...]
    lane = lax.broadcasted_iota(jnp.int32, (tq, LANES), 1)
    first = lane < HEAD_DIM
    qh = (jnp.where(first, q2, jnp.zeros_like(q2)), jnp.where(first, jnp.zeros_like(q2), q2))
    b2 = (bias_ref[2 * hp] * LOG2E, bias_ref[2 * hp + 1] * LOG2E)
    upper = _strict_upper(tk)
    acc_ref[...] = jnp.zeros_like(acc_ref)
    carry_ref[...] = jnp.zeros_like(carry_ref)

    def block(kb, mask):
        kblk = k_ref[kb]
        vblk = v_ref[kb]
        pv = []
        for hh in range(2):
            z = _dot(qh[hh], kblk) + b2[hh]
            a, new_carry = _sb_block(z, upper, carry_ref[hh], mask)
            carry_ref[hh] = new_carry
            pv.append(_dot_nt(a, vblk))
        acc_ref[...] += jnp.where(first, pv[0], pv[1])

    trow = lax.broadcasted_iota(jnp.int32, (tq, tk), 0)
    scol = lax.broadcasted_iota(jnp.int32, (tq, tk), 1)
    for i in reversed(range(ratio)):
        block(qb * ratio + i, scol + i * tk < trow)

    def body(i, c):
        block(qb * ratio - 1 - i, None)
        return c
    lax.fori_loop(0, qb * ratio, body, 0)
    o_ref[...] = acc_ref[...].astype(o_ref.dtype)


def _sb_prompt(q, kt, vt, bias, tq):
    m, att_w = q.shape
    n_seq, n_kblocks, _, tk = kt.shape
    seq = n_kblocks * tk
    n_pairs = att_w // LANES
    qblocks = seq // tq
    qspec = pl.BlockSpec((tq, LANES), lambda n, hp, qb: (n * qblocks + qb, hp))
    kvspec = pl.BlockSpec((None, n_kblocks, LANES, tk), lambda n, hp, qb: (n, 0, hp, 0))
    return pl.pallas_call(
        functools.partial(_sb_prompt_kernel, tq=tq, tk=tk),
        grid=(n_seq, n_pairs, qblocks),
        in_specs=[pl.BlockSpec(memory_space=pltpu.SMEM), qspec, kvspec, kvspec],
        out_specs=qspec,
        out_shape=jax.ShapeDtypeStruct((m, att_w), BF16),
        scratch_shapes=[pltpu.VMEM((tq, LANES), F32), pltpu.VMEM((2, tq, 1), F32)],
        compiler_params=pltpu.CompilerParams(dimension_semantics=("parallel", "parallel", "arbitrary")),
        name="sb_attention_prompt",
    )(bias, q, kt, vt)


def _sb_decode_kernel(pt_ref, bias_ref, qx_ref, kn_ref, vn_ref, *rest, pages_per_step, n_steps):
    del pt_ref
    kpages = rest[:pages_per_step]
    vpages = rest[pages_per_step:2 * pages_per_step]
    o_ref, acc_ref, carry_ref = rest[2 * pages_per_step:]
    j = pl.program_id(1)
    qx = qx_ref[...]
    rows = qx.shape[0]
    page = kn_ref.shape[1]
    dec_seq = o_ref.shape[0] // 2
    b2 = bias_ref[...] * LOG2E
    upper = _strict_upper(page)

    @pl.when(j == 0)
    def _():
        trow = lax.broadcasted_iota(jnp.int32, (rows, page), 0) & (dec_seq - 1)
        scol = lax.broadcasted_iota(jnp.int32, (rows, page), 1)
        z = _dot(qx, kn_ref[...]) + b2
        a, carry = _sb_block(z, upper, jnp.zeros((rows, 1), F32), scol < trow)
        carry_ref[...] = carry
        acc_ref[...] = _dot_nt(a, vn_ref[...])

    carry = carry_ref[...]
    acc = acc_ref[...]
    for i in range(pages_per_step):
        z = _dot(qx, kpages[i][...].astype(BF16)) + b2
        a, carry = _sb_block(z, upper, carry, None)
        acc = acc + _dot_nt(a, vpages[i][...].astype(BF16))
    carry_ref[...] = carry
    acc_ref[...] = acc

    @pl.when(j == n_steps - 1)
    def _():
        head_of_row = lax.broadcasted_iota(jnp.int32, acc.shape, 0) // dec_seq
        head_of_lane = lax.broadcasted_iota(jnp.int32, acc.shape, 1) // HEAD_DIM
        own = jnp.where(head_of_row == head_of_lane, acc, 0.0).astype(BF16)
        t_out = lax.broadcasted_iota(jnp.int32, (o_ref.shape[0], rows), 0)
        r_in = lax.broadcasted_iota(jnp.int32, (o_ref.shape[0], rows), 1)
        pick = jnp.where((r_in & (dec_seq - 1)) == t_out, 1.0, 0.0).astype(BF16)
        o_ref[...] = _dot(pick, own).astype(o_ref.dtype)


def _sb_decode(qx, kt_new, vt_new, bias_rows, cache_kt, cache_vt, layer, page_table, pages_per_step):
    n_dec, rows, att_w = qx.shape
    page = cache_kt.shape[3]
    n_pages = page_table.shape[1]
    n_steps = n_pages // pages_per_step
    dec_seq = rows // (att_w // HEAD_DIM)

    def page_spec(i):
        return pl.BlockSpec(
            (None, None, att_w, page),
            lambda b, j, pt: (layer, pt[b, n_pages - 1 - (j * pages_per_step + i)], 0, 0))

    per_b = lambda r, c: pl.BlockSpec((None, r, c), lambda b, j, pt: (b, 0, 0))
    grid_spec = pltpu.PrefetchScalarGridSpec(
        num_scalar_prefetch=1,
        grid=(n_dec, n_steps),
        in_specs=([pl.BlockSpec(bias_rows.shape, lambda b, j, pt: (0, 0)), per_b(rows, att_w),
                   per_b(att_w, page), per_b(att_w, page)]
                  + [page_spec(i) for i in range(pages_per_step)] * 2),
        out_specs=per_b(2 * dec_seq, att_w),
        scratch_shapes=[pltpu.VMEM((rows, att_w), F32), pltpu.VMEM((rows, 1), F32)],
    )
    return pl.pallas_call(
        functools.partial(_sb_decode_kernel, pages_per_step=pages_per_step, n_steps=n_steps),
        grid_spec=grid_spec,
        out_shape=jax.ShapeDtypeStruct((n_dec, 2 * dec_seq, att_w), BF16),
        compiler_params=pltpu.CompilerParams(dimension_semantics=("parallel", "arbitrary")),
        name="sb_attention_decode",
    )(page_table, bias_rows, qx, kt_new, vt_new, *([cache_kt] * pages_per_step), *([cache_vt] * pages_per_step))


def _merge_kernel(x_ref, nm_ref, wg_ref, att_ref, u_ref, gvn_ref, wsp_ref, bsp_ref, cbcy_ref,
                  wpa_ref, wpg_ref, wpc_ref, wout_ref, o_ref, *, sample_mode):
    x = x_ref[...]
    d = x.shape[1]
    xn = _rms_rows(x, nm_ref[...]).astype(BF16)
    tm = x.shape[0]
    gm_w = u_ref.shape[1]
    ug = []
    if sample_mode:
        n_tok = bsp_ref.shape[0]
        n_dec = tm // n_tok
        for t in range(n_tok):
            mixed = jnp.broadcast_to(bsp_ref[t:t + 1, :], (n_dec, gm_w))
            for s in range(t + 1):
                mixed = mixed + wsp_ref[t * n_tok + s:t * n_tok + s + 1, :] * gvn_ref[s * n_dec:(s + 1) * n_dec, :]
            ug.append((u_ref[t * n_dec:(t + 1) * n_dec, :] * mixed).astype(BF16))
    else:
        group_of_lane = lax.broadcasted_iota(jnp.int32, (CHUNK, gm_w), 1) // GM_GROUP_DIM
        for c in range(tm // CHUNK):
            sl = slice(c * CHUNK, (c + 1) * CHUNK)
            gv = gvn_ref[sl, :].astype(BF16)
            mixed = bsp_ref[...]
            for g in range(GM_GROUPS):
                mixed = mixed + jnp.where(group_of_lane == g, _dot(wsp_ref[g], gv), 0.0)
            ug.append((u_ref[sl, :] * mixed).astype(BF16))
    ug = jnp.concatenate(ug, axis=0)

    merged = jax.nn.sigmoid(_dot(xn, wg_ref[:, 0:d])) * _dot(att_ref[...], wpa_ref[...])
    merged = merged + jax.nn.sigmoid(_dot(xn, wg_ref[:, d:2 * d])) * _dot(ug, wpg_ref[...])
    merged = merged + jax.nn.sigmoid(_dot(xn, wg_ref[:, 2 * d:3 * d])) * _dot(cbcy_ref[...], wpc_ref[...])
    o_ref[...] = x + _dot(merged.astype(BF16), wout_ref[...])


def _merge(x, att, u, gvn, cbcy, w, wsp, bsp, tm, sample_mode):
    m, d = x.shape
    full = lambda a: pl.BlockSpec(a.shape, lambda i: (0,) * a.ndim)
    rows = lambda a: pl.BlockSpec((tm, a.shape[1]), lambda i: (i, 0))
    ins = [x, w["norm_mix"], w["wg"], att, u, gvn, wsp, bsp, cbcy, w["wpa"], w["wpg"], w["wpc"], w["wout"]]
    is_rows = [True, False, False, True, True, True, False, False, True, False, False, False, False]
    return pl.pallas_call(
        functools.partial(_merge_kernel, sample_mode=sample_mode), grid=(m // tm,),
        in_specs=[rows(a) if r else full(a) for a, r in zip(ins, is_rows)],
        out_specs=pl.BlockSpec((tm, d), lambda i: (i, 0)),
        out_shape=jax.ShapeDtypeStruct((m, d), F32),
        compiler_params=pltpu.CompilerParams(dimension_semantics=("parallel",)),
        name="merge_sample" if sample_mode else "merge_prompt",
    )(*ins)


def _mlp_kernel(x_ref, nm_ref, wup_ref, wdn_ref, o_ref):
    x = x_ref[...]
    xn = _rms_rows(x, nm_ref[...]).astype(BF16)
    acc = x
    for c in range(wup_ref.shape[1] // D_FF_CHUNK):
        sl = slice(c * D_FF_CHUNK, (c + 1) * D_FF_CHUNK)
        h = jnp.square(jnp.maximum(_dot(xn, wup_ref[:, sl]), 0.0)).astype(BF16)
        acc = acc + _dot(h, wdn_ref[sl, :])
    o_ref[...] = acc


def _mlp(x, w, tm):
    m, d = x.shape
    full = lambda a: pl.BlockSpec(a.shape, lambda i: (0,) * a.ndim)
    rows = pl.BlockSpec((tm, d), lambda i: (i, 0))
    return pl.pallas_call(
        _mlp_kernel, grid=(m // tm,),
        in_specs=[rows, full(w["norm_mlp"]), full(w["wup"]), full(w["wdn"])],
        out_specs=rows, out_shape=jax.ShapeDtypeStruct((m, d), F32),
        compiler_params=pltpu.CompilerParams(dimension_semantics=("parallel",)),
        name="mlp",
    )(x, w["norm_mlp"], w["wup"], w["wdn"])


def _blockdiag_ones(width, group):
    g = jnp.arange(width, dtype=jnp.int32) // group
    return (g[:, None] == g[None, :]).astype(BF16)


def _layer_weights(l, norm_mix, w_in, q_gain, k_gain, gm_gain, conv_w, w_proj_attn, w_proj_gmlp,
                   w_proj_conv, w_out, norm_mlp, w_up, w_down):
    d = w_in.shape[1]
    att_w = w_proj_attn.shape[1]
    gm_w = w_proj_gmlp.shape[1]
    conv_width = w_proj_conv.shape[1]
    n_heads = att_w // HEAD_DIM
    widths = [att_w] * 3 + [gm_w] * 2 + [conv_width] * 3 + [3 * d]
    offs = [0]
    for wd in widths:
        offs.append(offs[-1] + wd)
    wl = w_in[l].astype(BF16)
    names = ["wq", "wk", "wv", "wu", "wgv", "wcb", "wcc", "wcx", "wg"]
    w = {n: wl[:, offs[i]:offs[i + 1]] for i, n in enumerate(names)}
    w.update(
        wkt=w.pop("wk").T, wvt=w.pop("wv").T,
        norm_mix=norm_mix[l][None, :], norm_mlp=norm_mlp[l][None, :],
        q_gain=jnp.tile(q_gain[l], n_heads)[None, :],
        k_gain_t=jnp.broadcast_to(jnp.tile(k_gain[l], n_heads)[:, None], (att_w, LANES)),
        gm_gain=gm_gain[l][None, :], conv_w=conv_w[l],
        hs_att=_blockdiag_ones(att_w, HEAD_DIM), hs_gm=_blockdiag_ones(gm_w, GM_GROUP_DIM),
        wpa=w_proj_attn[l].astype(BF16), wpg=w_proj_gmlp[l].astype(BF16), wpc=w_proj_conv[l].astype(BF16),
        wout=w_out[l].astype(BF16), wup=w_up[l].astype(BF16), wdn=w_down[l].astype(BF16))
    return w


def _spatial_weights(w_spatial_l, b_spatial_l, dec_seq):
    r = jnp.arange(CHUNK, dtype=jnp.int32)
    tril = r[:, None] >= r[None, :]
    wsp_prompt = jnp.where(tril[None], w_spatial_l, 0.0).astype(BF16)
    bsp_prompt = jnp.repeat(b_spatial_l.T, GM_GROUP_DIM, axis=1)
    corner = w_spatial_l[:, :dec_seq, :dec_seq].reshape(GM_GROUPS, dec_seq * dec_seq)
    wsp_sample = jnp.repeat(corner.T, GM_GROUP_DIM, axis=1)
    bsp_sample = jnp.repeat(b_spatial_l[:, :dec_seq].T, GM_GROUP_DIM, axis=1)
    return wsp_prompt, bsp_prompt, wsp_sample, bsp_sample


def kernel(x_prompt, x_sample, cache_k, cache_v, state_conv, page_table, norm_mix, w_in, q_gain, k_gain, sb_bias, gm_gain, w_spatial, b_spatial, conv_w, w_proj_attn, w_proj_gmlp, w_proj_conv, w_out, norm_mlp, w_up, w_down):
    n_seq, seq, d = x_prompt.shape
    n_dec, dec_seq, _ = x_sample.shape
    depth, n_pool, page, n_heads, head_dim = cache_k.shape
    assert head_dim == HEAD_DIM and page == CHUNK and dec_seq == 4 and seq % 512 == 0 and n_dec % 8 == 0
    att_w = n_heads * head_dim
    conv_width = state_conv.shape[-1]
    n_pages = page_table.shape[1]
    tm_prompt = 512
    tq, tk = 256, 256
    pages_per_step = 8 if n_pages % 8 == 0 else n_pages
    m_s = n_dec * dec_seq

    cache_kt = cache_k.transpose(0, 1, 3, 4, 2).reshape(depth, n_pool, att_w, page)
    cache_vt = cache_v.transpose(0, 1, 3, 4, 2).reshape(depth, n_pool, att_w, page)
    xp = x_prompt.reshape(n_seq * seq, d)
    xs = x_sample.transpose(1, 0, 2).reshape(m_s, d)
    head_of_lane = jnp.arange(att_w, dtype=jnp.int32) // HEAD_DIM
    own_head = head_of_lane[None, None, :] == jnp.arange(n_heads, dtype=jnp.int32)[None, :, None]

    outs = {name: [] for name in ("kp", "vp", "cp", "ks", "vs", "cs", "gs")}
    for l in range(depth):
        w = _layer_weights(l, norm_mix, w_in, q_gain, k_gain, gm_gain, conv_w, w_proj_attn,
                           w_proj_gmlp, w_proj_conv, w_out, norm_mlp, w_up, w_down)
        wsp_p, bsp_p, wsp_s, bsp_s = _spatial_weights(w_spatial[l], b_spatial[l], dec_seq)

        q, kf, kb, vf, vb, u, gvn, xc, cbcy = _in_proj(xp, w, None, tm_prompt, n_seq, tk)
        att = _sb_prompt(q, kb, vb, sb_bias[l], tq)
        xp = _mlp(_merge(xp, att, u, gvn, cbcy, w, wsp_p, bsp_p, tm_prompt, False), w, tm_prompt)
        outs["kp"].append(kf)
        outs["vp"].append(vf)
        outs["cp"].append(xc.reshape(n_seq, seq, conv_width)[:, seq - (CONV_K - 1):])

        q, kf, kb, vf, vb, u, gvn, xc, cbcy = _in_proj(
            xs, w, (state_conv[l][:, 0], state_conv[l][:, 1]), n_dec, dec_seq, page)
        q3 = q.reshape(dec_seq, n_dec, att_w).transpose(1, 0, 2)
        qx = jnp.where(own_head[:, :, None, :], q3[:, None], jnp.zeros((), BF16))
        qx = qx.reshape(n_dec, n_heads * dec_seq, att_w)
        new_page = lambda a: jnp.pad(a.reshape(dec_seq, att_w, n_dec).transpose(2, 1, 0),
                                     ((0, 0), (0, 0), (0, page - dec_seq)))
        bias_rows = jnp.repeat(sb_bias[l], dec_seq)[:, None]
        att = _sb_decode(qx, new_page(kb), new_page(vb), bias_rows, cache_kt, cache_vt, l, page_table,
                         pages_per_step)
        att = att[:, :dec_seq].transpose(1, 0, 2).reshape(m_s, att_w)
        xs = _mlp(_merge(xs, att, u, gvn, cbcy, w, wsp_s, bsp_s, m_s, True), w, m_s)
        outs["ks"].append(kf)
        outs["vs"].append(vf)
        outs["cs"].append(xc.reshape(dec_seq, n_dec, conv_width)[dec_seq - (CONV_K - 1):].transpose(1, 0, 2))
        outs["gs"].append(gvn.reshape(dec_seq, n_dec, -1).transpose(1, 0, 2))

    stack = lambda name: jnp.stack(outs[name])
    prompt_kv = lambda name: stack(name).reshape(depth, n_seq, n_heads, head_dim, seq).transpose(0, 1, 4, 2, 3)
    sample_kv = lambda name: stack(name).reshape(depth, dec_seq, n_heads, head_dim, n_dec).transpose(0, 4, 1, 2, 3)
    return (xp.reshape(n_seq, seq, d), xs.reshape(dec_seq, n_dec, d).transpose(1, 0, 2),
            prompt_kv("kp"), prompt_kv("vp"), stack("cp"),
            sample_kv("ks"), sample_kv("vs"), stack("cs"), stack("gs"))
```

```python
import functools

import jax
import jax.numpy as jnp
from jax import lax
from jax.experimental import pallas as pl
from jax.experimental.pallas import tpu as pltpu

HEAD_DIM = 64
GM_GROUP_DIM = 64
GM_GROUPS = 4
CONV_K = 3
CHUNK = 128
EPS = 1e-6
LOG2E = 1.4426950408889634
LN2 = 0.6931471805599453
LANES = 128
ROW_TILE = 512
Q_TILE = 512
K_TILE = 256
PAGES_PER_STEP = 8
D_FF_CHUNK = 1024
MASKED_LOG2 = -1e30
BF16 = jnp.bfloat16
F32 = jnp.float32


def _dot(a, b):
    return jnp.dot(a, b, preferred_element_type=F32)


def _dot_nt(a, b):
    return lax.dot_general(a, b, (((1,), (1,)), ((), ())), preferred_element_type=F32)


def _rms_rows(x, gain):
    ms = jnp.mean(x * x, axis=-1, keepdims=True)
    return x * lax.rsqrt(ms + EPS) * gain


def _group_rms_lanes(y, ones_blockdiag, group):
    sq = y * y
    hi = sq.astype(BF16)
    lo = (sq - hi.astype(F32)).astype(BF16)
    ss = _dot(hi, ones_blockdiag) + _dot(lo, ones_blockdiag)
    return y * lax.rsqrt(ss * (1.0 / group) + EPS)


def _group_rms_rows(yt, group):
    f, n = yt.shape
    y3 = yt.reshape(f // group, group, n)
    ss = jnp.sum(y3 * y3, axis=1, keepdims=True)
    return (y3 * lax.rsqrt(ss * (1.0 / group) + EPS)).reshape(f, n)


def _tile_lanes(a, n):
    return a[:, :n] if n <= a.shape[1] else jnp.concatenate([a] * (n // a.shape[1]), axis=1)


def _in_proj_kernel(*refs, sample_mode, tiles_per_seq):
    if sample_mode:
        (x_ref, nm_ref, wq, wkt, wvt, wu, wgv, wcb, wcc, wcx, hs_att, hs_gm, qg, kgt, gmg, cw,
         buf0_ref, buf1_ref, q_o, kf_o, kb_o, vf_o, vb_o, u_o, gvn_o, xc_o, cbcy_o, prev1_ref, prev2_ref) = refs
    else:
        (x_ref, nm_ref, wq, wkt, wvt, wu, wgv, wcb, wcc, wcx, hs_att, hs_gm, qg, kgt, gmg, cw,
         q_o, kf_o, kb_o, vf_o, vb_o, u_o, gvn_o, xc_o, cbcy_o, tail_ref) = refs

    xn = _rms_rows(x_ref[...], nm_ref[...]).astype(BF16)
    tm = xn.shape[0]
    tks = kb_o.shape[2]

    q = _group_rms_lanes(_dot(xn, wq[...]), hs_att[...], HEAD_DIM) * (qg[...] * (LOG2E * HEAD_DIM ** -0.5))
    q_o[...] = q.astype(BF16)
    kt = _group_rms_rows(_dot_nt(wkt[...], xn), HEAD_DIM) * _tile_lanes(kgt[...], tm)
    vt = _dot_nt(wvt[...], xn)
    kf_o[...] = kt
    vf_o[...] = vt
    for j in range(tm // tks):
        kb_o[j] = kt[:, j * tks:(j + 1) * tks].astype(BF16)
        vb_o[j] = vt[:, j * tks:(j + 1) * tks].astype(BF16)
    u_o[...] = _dot(xn, wu[...])
    gvn_o[...] = _group_rms_lanes(_dot(xn, wgv[...]), hs_gm[...], GM_GROUP_DIM) * gmg[...]

    cb = _dot(xn, wcb[...])
    xc = _dot(xn, wcc[...]) * _dot(xn, wcx[...])
    xc_o[...] = xc
    if sample_mode:
        @pl.when(pl.program_id(0) == 0)
        def _():
            prev2_ref[...] = buf0_ref[...]
            prev1_ref[...] = buf1_ref[...]
        prev1 = prev1_ref[...]
        prev2 = prev2_ref[...]
        prev2_ref[...] = prev1
        prev1_ref[...] = xc
    else:
        @pl.when(pl.program_id(0) % tiles_per_seq == 0)
        def _():
            tail_ref[...] = jnp.zeros_like(tail_ref)
        row = lax.broadcasted_iota(jnp.int32, xc.shape, 0)
        last2 = jnp.broadcast_to(tail_ref[0:1, :], xc.shape)
        last1 = jnp.broadcast_to(tail_ref[1:2, :], xc.shape)
        prev1 = jnp.where(row == 0, last1, pltpu.roll(xc, 1, axis=0))
        prev2 = jnp.where(row == 0, last2, jnp.where(row == 1, last1, pltpu.roll(xc, 2, axis=0)))
        tail_ref[0:2, :] = xc[tm - 2:tm, :]
    cy = prev2 * cw[0:1, :] + prev1 * cw[1:2, :] + xc * cw[2:3, :]
    cbcy_o[...] = (cb * cy).astype(BF16)


def _in_proj(x, w, conv_state, tm, n_groups, tk_store):
    m, d = x.shape
    sample_mode = conv_state is not None
    att_w = w["wq"].shape[1]
    gm_w = w["wu"].shape[1]
    group_len = m // n_groups
    tps = group_len // tm
    tks = min(tk_store, tm)
    full = lambda a: pl.BlockSpec(a.shape, lambda i: (0,) * a.ndim)
    rows = lambda width: pl.BlockSpec((tm, width), lambda i: (i, 0))
    ins = [x, w["norm_mix"], w["wq"], w["wkt"], w["wvt"], w["wu"], w["wgv"], w["wcb"], w["wcc"], w["wcx"],
           w["hs_att"], w["hs_gm"], w["q_gain"], w["k_gain_t"], w["gm_gain"], w["conv_w"]]
    in_specs = [rows(d)] + [full(a) for a in ins[1:]]
    if sample_mode:
        ins += list(conv_state)
        in_specs += [full(conv_state[0]), full(conv_state[1])]
    ft_f32 = jax.ShapeDtypeStruct((n_groups, att_w, group_len), F32)
    ft_bf16 = jax.ShapeDtypeStruct((n_groups, group_len // tks, att_w, tks), BF16)
    ft_f32_spec = pl.BlockSpec((None, att_w, tm), lambda i: (i // tps, 0, i % tps))
    ft_bf16_spec = pl.BlockSpec((None, tm // tks, att_w, tks), lambda i: (i // tps, i % tps, 0, 0))
    out_shape = [jax.ShapeDtypeStruct((m, att_w), BF16),
                 ft_f32, ft_bf16,
                 ft_f32, ft_bf16,
                 jax.ShapeDtypeStruct((m, gm_w), F32),
                 jax.ShapeDtypeStruct((m, gm_w), F32),
                 jax.ShapeDtypeStruct((m, gm_w), F32),
                 jax.ShapeDtypeStruct((m, gm_w), BF16)]
    out_specs = [rows(att_w), ft_f32_spec, ft_bf16_spec, ft_f32_spec, ft_bf16_spec,
                 rows(gm_w), rows(gm_w), rows(gm_w), rows(gm_w)]
    if sample_mode:
        scratch = [pltpu.VMEM((tm, gm_w), F32), pltpu.VMEM((tm, gm_w), F32)]
    else:
        scratch = [pltpu.VMEM((8, gm_w), F32)]
    return pl.pallas_call(
        functools.partial(_in_proj_kernel, sample_mode=sample_mode, tiles_per_seq=tps),
        grid=(m // tm,), in_specs=in_specs, out_specs=out_specs, out_shape=out_shape,
        scratch_shapes=scratch,
        compiler_params=pltpu.CompilerParams(dimension_semantics=("arbitrary",)),
        name="in_proj_sample" if sample_mode else "in_proj_prompt",
    )(*ins)


def _neg_log2_rest(z):
    neg_abs = lax.bitcast_convert_type(lax.bitcast_convert_type(z, jnp.uint32) | jnp.uint32(0x80000000), F32)
    return jnp.maximum(z, 0.0) + jnp.log(1.0 + jnp.exp2(neg_abs)) * (1.0 / LN2)


def _neg_suffix_ones(n):
    j = lax.broadcasted_iota(jnp.int32, (n, n), 0)
    s = lax.broadcasted_iota(jnp.int32, (n, n), 1)
    return jnp.where(j >= s, -1.0, 0.0).astype(BF16)


def _sb_weights(z, p_bf16, neg_suffix, carry):
    cs = _dot(p_bf16, neg_suffix)
    return jnp.exp2(z + cs + carry).astype(BF16), carry + cs[:, 0:1]


def _sb_prompt_kernel(bias_ref, q_ref, k_ref, v_ref, o_ref, acc_ref, carry_ref, z_ref, p_ref, a_ref, *, tq, tk):
    hp = pl.program_id(1)
    qb = pl.program_id(2)
    ratio = tq // tk
    n_stages = 4
    steady_start = max(ratio, n_stages - 1)
    assert 2 * ratio >= steady_start
    n_blocks = (qb + 1) * ratio
    q2 = q_ref[...]
    first = lax.broadcasted_iota(jnp.int32, (tq, LANES), 1) < HEAD_DIM
    qh = (jnp.where(first, q2, jnp.zeros_like(q2)), jnp.where(first, jnp.zeros_like(q2), q2))
    b2 = (bias_ref[2 * hp] * LOG2E, bias_ref[2 * hp + 1] * LOG2E)
    neg_suffix = _neg_suffix_ones(tk)
    acc_ref[...] = jnp.zeros_like(acc_ref)
    carry_ref[...] = jnp.zeros_like(carry_ref)

    def stage0(t):
        kblk = k_ref[n_blocks - 1 - t]
        for hh in range(2):
            z = _dot(qh[hh], kblk) + b2[hh]
            if isinstance(t, int) and t < ratio:
                trow = lax.broadcasted_iota(jnp.int32, (tq, tk), 0)
                scol = lax.broadcasted_iota(jnp.int32, (tq, tk), 1) + (ratio - 1 - t) * tk
                z = jnp.where(scol < trow, z, MASKED_LOG2)
            z_ref[t & 3, hh] = z

    def stage1(t):
        for hh in range(2):
            p_ref[t & 1, hh] = _neg_log2_rest(z_ref[t & 3, hh]).astype(BF16)

    def stage2(t):
        for hh in range(2):
            a_ref[t & 1, hh], carry_ref[hh] = _sb_weights(z_ref[t & 3, hh], p_ref[t & 1, hh], neg_suffix,
                                                          carry_ref[hh])

    def stage3(t):
        vblk = v_ref[n_blocks - 1 - t]
        pv = [_dot_nt(a_ref[t & 1, hh], vblk) for hh in range(2)]
        acc_ref[...] += jnp.where(first, pv[0], pv[1])

    stages = (stage0, stage1, stage2, stage3)

    def iteration(t, first_stage=0, last_stage=n_stages - 1):
        for s in reversed(range(first_stage, last_stage + 1)):
            stages[s](t - s)

    @pl.when(qb == 0)
    def _():
        for t in range(ratio):
            for stage in stages:
                stage(t)

    @pl.when(qb > 0)
    def _():
        for t in range(steady_start):
            iteration(t, last_stage=min(t, n_stages - 1))

        def body(t, c):
            iteration(t)
            return c
        lax.fori_loop(steady_start, n_blocks, body, 0)
        for i in range(1, n_stages):
            iteration(n_blocks - 1 + i, first_stage=i)

    o_ref[...] = acc_ref[...].astype(o_ref.dtype)


def _sb_prompt(q, kt, vt, bias, tq):
    m, att_w = q.shape
    n_seq, n_kblocks, _, tk = kt.shape
    seq = n_kblocks * tk
    n_pairs = att_w // LANES
    qblocks = seq // tq
    qspec = pl.BlockSpec((tq, LANES), lambda n, hp, qb: (n * qblocks + qb, hp))
    kvspec = pl.BlockSpec((None, n_kblocks, LANES, tk), lambda n, hp, qb: (n, 0, hp, 0))
    return pl.pallas_call(
        functools.partial(_sb_prompt_kernel, tq=tq, tk=tk),
        grid=(n_seq, n_pairs, qblocks),
        in_specs=[pl.BlockSpec(memory_space=pltpu.SMEM), qspec, kvspec, kvspec],
        out_specs=qspec,
        out_shape=jax.ShapeDtypeStruct((m, att_w), BF16),
        scratch_shapes=[pltpu.VMEM((tq, LANES), F32),
                        pltpu.VMEM((2, tq, 1), F32),
                        pltpu.VMEM((4, 2, tq, tk), F32),
                        pltpu.VMEM((2, 2, tq, tk), BF16),
                        pltpu.VMEM((2, 2, tq, tk), BF16)],
        compiler_params=pltpu.CompilerParams(dimension_semantics=("parallel", "parallel", "arbitrary")),
        name="sb_attention_prompt",
    )(bias, q, kt, vt)


def _sb_decode_kernel(pt_ref, bias_ref, qx_ref, kn_ref, vn_ref, *rest, pages_per_step, n_steps):
    del pt_ref
    kpages = rest[:pages_per_step]
    vpages = rest[pages_per_step:2 * pages_per_step]
    o_ref, acc_ref, carry_ref = rest[2 * pages_per_step:]
    j = pl.program_id(1)
    qx = qx_ref[...]
    rows = qx.shape[0]
    page = kn_ref.shape[1]
    dec_seq = o_ref.shape[0] // 2
    b2 = bias_ref[...] * LOG2E
    neg_suffix = _neg_suffix_ones(page)

    @pl.when(j == 0)
    def _():
        trow = lax.broadcasted_iota(jnp.int32, (rows, page), 0) & (dec_seq - 1)
        scol = lax.broadcasted_iota(jnp.int32, (rows, page), 1)
        z = jnp.where(scol < trow, _dot(qx, kn_ref[...]) + b2, MASKED_LOG2)
        a, carry_ref[...] = _sb_weights(z, _neg_log2_rest(z).astype(BF16), neg_suffix, jnp.zeros((rows, 1), F32))
        acc_ref[...] = _dot_nt(a, vn_ref[...])

    z = jnp.concatenate([_dot(qx, kpages[i][...].astype(BF16)) + b2 for i in range(pages_per_step)], axis=0)
    cs = _dot(_neg_log2_rest(z).astype(BF16), neg_suffix)
    carries = [carry_ref[...]]
    for i in range(pages_per_step):
        carries.append(carries[-1] + cs[i * rows:(i + 1) * rows, 0:1])
    a = jnp.exp2(z + cs + jnp.concatenate(carries[:-1], axis=0)).astype(BF16)
    acc = acc_ref[...]
    for i in range(pages_per_step):
        acc = acc + _dot_nt(a[i * rows:(i + 1) * rows], vpages[i][...].astype(BF16))
    carry_ref[...] = carries[-1]
    acc_ref[...] = acc

    @pl.when(j == n_steps - 1)
    def _():
        head_of_row = lax.broadcasted_iota(jnp.int32, acc.shape, 0) // dec_seq
        head_of_lane = lax.broadcasted_iota(jnp.int32, acc.shape, 1) // HEAD_DIM
        own = jnp.where(head_of_row == head_of_lane, acc, 0.0).astype(BF16)
        t_out = lax.broadcasted_iota(jnp.int32, (o_ref.shape[0], rows), 0)
        r_in = lax.broadcasted_iota(jnp.int32, (o_ref.shape[0], rows), 1)
        pick = jnp.where((r_in & (dec_seq - 1)) == t_out, 1.0, 0.0).astype(BF16)
        o_ref[...] = _dot(pick, own).astype(o_ref.dtype)


def _sb_decode(qx, kt_new, vt_new, bias_rows, cache_kt, cache_vt, layer, page_table, pages_per_step):
    n_dec, rows, att_w = qx.shape
    page = cache_kt.shape[3]
    n_pages = page_table.shape[1]
    n_steps = n_pages // pages_per_step
    dec_seq = rows // (att_w // HEAD_DIM)

    def page_spec(i):
        return pl.BlockSpec(
            (None, None, att_w, page),
            lambda b, j, pt: (layer, pt[b, n_pages - 1 - (j * pages_per_step + i)], 0, 0))

    per_b = lambda r, c: pl.BlockSpec((None, r, c), lambda b, j, pt: (b, 0, 0))
    grid_spec = pltpu.PrefetchScalarGridSpec(
        num_scalar_prefetch=1,
        grid=(n_dec, n_steps),
        in_specs=([pl.BlockSpec(bias_rows.shape, lambda b, j, pt: (0, 0)), per_b(rows, att_w),
                   per_b(att_w, page), per_b(att_w, page)]
                  + [page_spec(i) for i in range(pages_per_step)] * 2),
        out_specs=per_b(2 * dec_seq, att_w),
        scratch_shapes=[pltpu.VMEM((rows, att_w), F32), pltpu.VMEM((rows, 1), F32)],
    )
    return pl.pallas_call(
        functools.partial(_sb_decode_kernel, pages_per_step=pages_per_step, n_steps=n_steps),
        grid_spec=grid_spec,
        out_shape=jax.ShapeDtypeStruct((n_dec, 2 * dec_seq, att_w), BF16),
        compiler_params=pltpu.CompilerParams(dimension_semantics=("parallel", "arbitrary")),
        name="sb_attention_decode",
    )(page_table, bias_rows, qx, kt_new, vt_new, *([cache_kt] * pages_per_step), *([cache_vt] * pages_per_step))


def _merge_kernel(x_ref, nm_ref, wg_ref, att_ref, u_ref, gvn_ref, wsp_ref, bsp_ref, cbcy_ref,
                  wpa_ref, wpg_ref, wpc_ref, wout_ref, o_ref, *, sample_mode):
    x = x_ref[...]
    d = x.shape[1]
    xn = _rms_rows(x, nm_ref[...]).astype(BF16)
    tm = x.shape[0]
    gm_w = u_ref.shape[1]
    ug = []
    if sample_mode:
        n_tok = bsp_ref.shape[0]
        n_dec = tm // n_tok
        for t in range(n_tok):
            mixed = jnp.broadcast_to(bsp_ref[t:t + 1, :], (n_dec, gm_w))
            for s in range(t + 1):
                mixed = mixed + wsp_ref[t * n_tok + s:t * n_tok + s + 1, :] * gvn_ref[s * n_dec:(s + 1) * n_dec, :]
            ug.append((u_ref[t * n_dec:(t + 1) * n_dec, :] * mixed).astype(BF16))
    else:
        group_of_lane = lax.broadcasted_iota(jnp.int32, (CHUNK, gm_w), 1) // GM_GROUP_DIM
        for c in range(tm // CHUNK):
            sl = slice(c * CHUNK, (c + 1) * CHUNK)
            gv = gvn_ref[sl, :].astype(BF16)
            mixed = bsp_ref[...]
            for g in range(GM_GROUPS):
                mixed = mixed + jnp.where(group_of_lane == g, _dot(wsp_ref[g], gv), 0.0)
            ug.append((u_ref[sl, :] * mixed).astype(BF16))
    ug = jnp.concatenate(ug, axis=0)

    merged = jax.nn.sigmoid(_dot(xn, wg_ref[:, 0:d])) * _dot(att_ref[...], wpa_ref[...])
    merged = merged + jax.nn.sigmoid(_dot(xn, wg_ref[:, d:2 * d])) * _dot(ug, wpg_ref[...])
    merged = merged + jax.nn.sigmoid(_dot(xn, wg_ref[:, 2 * d:3 * d])) * _dot(cbcy_ref[...], wpc_ref[...])
    o_ref[...] = x + _dot(merged.astype(BF16), wout_ref[...])


def _merge(x, att, u, gvn, cbcy, w, wsp, bsp, tm, sample_mode):
    m, d = x.shape
    full = lambda a: pl.BlockSpec(a.shape, lambda i: (0,) * a.ndim)
    rows = lambda a: pl.BlockSpec((tm, a.shape[1]), lambda i: (i, 0))
    ins = [x, w["norm_mix"], w["wg"], att, u, gvn, wsp, bsp, cbcy, w["wpa"], w["wpg"], w["wpc"], w["wout"]]
    is_rows = [True, False, False, True, True, True, False, False, True, False, False, False, False]
    return pl.pallas_call(
        functools.partial(_merge_kernel, sample_mode=sample_mode), grid=(m // tm,),
        in_specs=[rows(a) if r else full(a) for a, r in zip(ins, is_rows)],
        out_specs=pl.BlockSpec((tm, d), lambda i: (i, 0)),
        out_shape=jax.ShapeDtypeStruct((m, d), F32),
        compiler_params=pltpu.CompilerParams(dimension_semantics=("parallel",)),
        name="merge_sample" if sample_mode else "merge_prompt",
    )(*ins)


def _mlp_kernel(x_ref, nm_ref, wup_ref, wdn_ref, o_ref):
    x = x_ref[...]
    xn = _rms_rows(x, nm_ref[...]).astype(BF16)
    acc = x
    for c in range(wup_ref.shape[1] // D_FF_CHUNK):
        sl = slice(c * D_FF_CHUNK, (c + 1) * D_FF_CHUNK)
        h = jnp.square(jnp.maximum(_dot(xn, wup_ref[:, sl]), 0.0)).astype(BF16)
        acc = acc + _dot(h, wdn_ref[sl, :])
    o_ref[...] = acc


def _mlp(x, w, tm):
    m, d = x.shape
    full = lambda a: pl.BlockSpec(a.shape, lambda i: (0,) * a.ndim)
    rows = pl.BlockSpec((tm, d), lambda i: (i, 0))
    return pl.pallas_call(
        _mlp_kernel, grid=(m // tm,),
        in_specs=[rows, full(w["norm_mlp"]), full(w["wup"]), full(w["wdn"])],
        out_specs=rows, out_shape=jax.ShapeDtypeStruct((m, d), F32),
        compiler_params=pltpu.CompilerParams(dimension_semantics=("parallel",)),
        name="mlp",
    )(x, w["norm_mlp"], w["wup"], w["wdn"])


def _blockdiag_ones(width, group):
    g = jnp.arange(width, dtype=jnp.int32) // group
    return (g[:, None] == g[None, :]).astype(BF16)


def _layer_weights(l, norm_mix, w_in, q_gain, k_gain, gm_gain, conv_w, w_proj_attn, w_proj_gmlp,
                   w_proj_conv, w_out, norm_mlp, w_up, w_down):
    d = w_in.shape[1]
    att_w = w_proj_attn.shape[1]
    gm_w = w_proj_gmlp.shape[1]
    conv_width = w_proj_conv.shape[1]
    n_heads = att_w // HEAD_DIM
    widths = [att_w] * 3 + [gm_w] * 2 + [conv_width] * 3 + [3 * d]
    offs = [0]
    for wd in widths:
        offs.append(offs[-1] + wd)
    wl = w_in[l].astype(BF16)
    names = ["wq", "wk", "wv", "wu", "wgv", "wcb", "wcc", "wcx", "wg"]
    w = {n: wl[:, offs[i]:offs[i + 1]] for i, n in enumerate(names)}
    w.update(
        wkt=w.pop("wk").T, wvt=w.pop("wv").T,
        norm_mix=norm_mix[l][None, :], norm_mlp=norm_mlp[l][None, :],
        q_gain=jnp.tile(q_gain[l], n_heads)[None, :],
        k_gain_t=jnp.broadcast_to(jnp.tile(k_gain[l], n_heads)[:, None], (att_w, LANES)),
        gm_gain=gm_gain[l][None, :], conv_w=conv_w[l],
        hs_att=_blockdiag_ones(att_w, HEAD_DIM), hs_gm=_blockdiag_ones(gm_w, GM_GROUP_DIM),
        wpa=w_proj_attn[l].astype(BF16), wpg=w_proj_gmlp[l].astype(BF16), wpc=w_proj_conv[l].astype(BF16),
        wout=w_out[l].astype(BF16), wup=w_up[l].astype(BF16), wdn=w_down[l].astype(BF16))
    return w


def _spatial_weights(w_spatial_l, b_spatial_l, dec_seq):
    r = jnp.arange(CHUNK, dtype=jnp.int32)
    tril = r[:, None] >= r[None, :]
    wsp_prompt = jnp.where(tril[None], w_spatial_l, 0.0).astype(BF16)
    bsp_prompt = jnp.repeat(b_spatial_l.T, GM_GROUP_DIM, axis=1)
    corner = w_spatial_l[:, :dec_seq, :dec_seq].reshape(GM_GROUPS, dec_seq * dec_seq)
    wsp_sample = jnp.repeat(corner.T, GM_GROUP_DIM, axis=1)
    bsp_sample = jnp.repeat(b_spatial_l[:, :dec_seq].T, GM_GROUP_DIM, axis=1)
    return wsp_prompt, bsp_prompt, wsp_sample, bsp_sample


def kernel(x_prompt, x_sample, cache_k, cache_v, state_conv, page_table, norm_mix, w_in, q_gain, k_gain, sb_bias, gm_gain, w_spatial, b_spatial, conv_w, w_proj_attn, w_proj_gmlp, w_proj_conv, w_out, norm_mlp, w_up, w_down):
    n_seq, seq, d = x_prompt.shape
    n_dec, dec_seq, _ = x_sample.shape
    depth, n_pool, page, n_heads, head_dim = cache_k.shape
    assert head_dim == HEAD_DIM and page == CHUNK and dec_seq == 4 and n_dec % 8 == 0
    att_w = n_heads * head_dim
    conv_width = state_conv.shape[-1]
    n_pages = page_table.shape[1]
    tm_prompt = min(ROW_TILE, seq)
    tq = min(Q_TILE, seq)
    assert seq % tm_prompt == 0 and seq % tq == 0 and tq % K_TILE == 0
    pages_per_step = PAGES_PER_STEP if n_pages % PAGES_PER_STEP == 0 else n_pages
    m_s = n_dec * dec_seq

    cache_kt = cache_k.transpose(0, 1, 3, 4, 2).reshape(depth, n_pool, att_w, page)
    cache_vt = cache_v.transpose(0, 1, 3, 4, 2).reshape(depth, n_pool, att_w, page)
    xp = x_prompt.reshape(n_seq * seq, d)
    xs = x_sample.transpose(1, 0, 2).reshape(m_s, d)
    head_of_lane = jnp.arange(att_w, dtype=jnp.int32) // HEAD_DIM
    own_head = head_of_lane[None, None, :] == jnp.arange(n_heads, dtype=jnp.int32)[None, :, None]

    outs = {name: [] for name in ("kp", "vp", "cp", "ks", "vs", "cs", "gs")}
    for l in range(depth):
        w = _layer_weights(l, norm_mix, w_in, q_gain, k_gain, gm_gain, conv_w, w_proj_attn,
                           w_proj_gmlp, w_proj_conv, w_out, norm_mlp, w_up, w_down)
        wsp_p, bsp_p, wsp_s, bsp_s = _spatial_weights(w_spatial[l], b_spatial[l], dec_seq)

        q, kf, kb, vf, vb, u, gvn, xc, cbcy = _in_proj(xp, w, None, tm_prompt, n_seq, K_TILE)
        att = _sb_prompt(q, kb, vb, sb_bias[l], tq)
        xp = _mlp(_merge(xp, att, u, gvn, cbcy, w, wsp_p, bsp_p, tm_prompt, False), w, tm_prompt)
        outs["kp"].append(kf)
        outs["vp"].append(vf)
        outs["cp"].append(xc.reshape(n_seq, seq, conv_width)[:, seq - (CONV_K - 1):])

        q, kf, kb, vf, vb, u, gvn, xc, cbcy = _in_proj(
            xs, w, (state_conv[l][:, 0], state_conv[l][:, 1]), n_dec, dec_seq, page)
        q3 = q.reshape(dec_seq, n_dec, att_w).transpose(1, 0, 2)
        qx = jnp.where(own_head[:, :, None, :], q3[:, None], jnp.zeros((), BF16))
        qx = qx.reshape(n_dec, n_heads * dec_seq, att_w)
        new_page = lambda a: jnp.pad(a.reshape(dec_seq, att_w, n_dec).transpose(2, 1, 0),
                                     ((0, 0), (0, 0), (0, page - dec_seq)))
        bias_rows = jnp.repeat(sb_bias[l], dec_seq)[:, None]
        att = _sb_decode(qx, new_page(kb), new_page(vb), bias_rows, cache_kt, cache_vt, l, page_table,
                         pages_per_step)
        att = att[:, :dec_seq].transpose(1, 0, 2).reshape(m_s, att_w)
        xs = _mlp(_merge(xs, att, u, gvn, cbcy, w, wsp_s, bsp_s, m_s, True), w, m_s)
        outs["ks"].append(kf)
        outs["vs"].append(vf)
        outs["cs"].append(xc.reshape(dec_seq, n_dec, conv_width)[dec_seq - (CONV_K - 1):].transpose(1, 0, 2))
        outs["gs"].append(gvn.reshape(dec_seq, n_dec, -1).transpose(1, 0, 2))

    stack = lambda name: jnp.stack(outs[name])
    prompt_kv = lambda name: stack(name).reshape(depth, n_seq, n_heads, head_dim, seq).transpose(0, 1, 4, 2, 3)
    sample_kv = lambda name: stack(name).reshape(depth, dec_seq, n_heads, head_dim, n_dec).transpose(0, 4, 1, 2, 3)
    return (xp.reshape(n_seq, seq, d), xs.reshape(dec_seq, n_dec, d).transpose(1, 0, 2),
            prompt_kv("kp"), prompt_kv("vp"), stack("cp"),
            sample_kv("ks"), sample_kv("vs"), stack("cs"), stack("gs"))
```

```python
import functools

import jax
import jax.numpy as jnp
from jax import lax
from jax.experimental import pallas as pl
from jax.experimental.pallas import tpu as pltpu

HEAD_DIM = 64
GM_GROUP_DIM = 64
GM_GROUPS = 4
CONV_K = 3
CHUNK = 128
EPS = 1e-6
LOG2E = 1.4426950408889634
LN2 = 0.6931471805599453
LANES = 128
ROW_TILE = 512
Q_TILE = 512
K_TILE = 256
BF16_SUBLANES = 16
D_FF_CHUNK = 1024
MASKED_LOG2 = -1e30
BF16 = jnp.bfloat16
F32 = jnp.float32


def _dot(a, b):
    return jnp.dot(a, b, preferred_element_type=F32)


def _dot_nt(a, b):
    return lax.dot_general(a, b, (((1,), (1,)), ((), ())), preferred_element_type=F32)


def _rms_rows(x, gain):
    ms = jnp.mean(x * x, axis=-1, keepdims=True)
    return x * lax.rsqrt(ms + EPS) * gain


def _group_rms_lanes(y, ones_blockdiag, group):
    sq = y * y
    hi = sq.astype(BF16)
    lo = (sq - hi.astype(F32)).astype(BF16)
    ss = _dot(hi, ones_blockdiag) + _dot(lo, ones_blockdiag)
    return y * lax.rsqrt(ss * (1.0 / group) + EPS)


def _group_rms_rows(yt, group):
    f, n = yt.shape
    y3 = yt.reshape(f // group, group, n)
    ss = jnp.sum(y3 * y3, axis=1, keepdims=True)
    return (y3 * lax.rsqrt(ss * (1.0 / group) + EPS)).reshape(f, n)


def _tile_lanes(a, n):
    return a[:, :n] if n <= a.shape[1] else jnp.concatenate([a] * (n // a.shape[1]), axis=1)


def _in_proj_kernel(*refs, sample_mode, tiles_per_seq):
    if sample_mode:
        (x_ref, nm_ref, wq, wkt, wvt, wu, wgv, wcb, wcc, wcx, hs_att, hs_gm, qg, kgt, gmg, cw,
         buf0_ref, buf1_ref, q_o, kf_o, kb_o, vf_o, vb_o, u_o, gvn_o, xc_o, cbcy_o, prev1_ref, prev2_ref) = refs
    else:
        (x_ref, nm_ref, wq, wkt, wvt, wu, wgv, wcb, wcc, wcx, hs_att, hs_gm, qg, kgt, gmg, cw,
         q_o, kf_o, kb_o, vf_o, vb_o, u_o, gvn_o, xc_o, cbcy_o, tail_ref) = refs

    xn = _rms_rows(x_ref[...], nm_ref[...]).astype(BF16)
    tm = xn.shape[0]
    tks = kb_o.shape[2]

    q = _group_rms_lanes(_dot(xn, wq[...]), hs_att[...], HEAD_DIM) * (qg[...] * (LOG2E * HEAD_DIM ** -0.5))
    q_o[...] = q.astype(BF16)
    kt = _group_rms_rows(_dot_nt(wkt[...], xn), HEAD_DIM) * _tile_lanes(kgt[...], tm)
    vt = _dot_nt(wvt[...], xn)
    kf_o[...] = kt
    vf_o[...] = vt
    for j in range(tm // tks):
        kb_o[j] = kt[:, j * tks:(j + 1) * tks].astype(BF16)
        vb_o[j] = vt[:, j * tks:(j + 1) * tks].astype(BF16)
    u_o[...] = _dot(xn, wu[...])
    gvn_o[...] = _group_rms_lanes(_dot(xn, wgv[...]), hs_gm[...], GM_GROUP_DIM) * gmg[...]

    cb = _dot(xn, wcb[...])
    xc = _dot(xn, wcc[...]) * _dot(xn, wcx[...])
    xc_o[...] = xc
    if sample_mode:
        @pl.when(pl.program_id(0) == 0)
        def _():
            prev2_ref[...] = buf0_ref[...]
            prev1_ref[...] = buf1_ref[...]
        prev1 = prev1_ref[...]
        prev2 = prev2_ref[...]
        prev2_ref[...] = prev1
        prev1_ref[...] = xc
    else:
        @pl.when(pl.program_id(0) % tiles_per_seq == 0)
        def _():
            tail_ref[...] = jnp.zeros_like(tail_ref)
        row = lax.broadcasted_iota(jnp.int32, xc.shape, 0)
        last2 = jnp.broadcast_to(tail_ref[0:1, :], xc.shape)
        last1 = jnp.broadcast_to(tail_ref[1:2, :], xc.shape)
        prev1 = jnp.where(row == 0, last1, pltpu.roll(xc, 1, axis=0))
        prev2 = jnp.where(row == 0, last2, jnp.where(row == 1, last1, pltpu.roll(xc, 2, axis=0)))
        tail_ref[0:2, :] = xc[tm - 2:tm, :]
    cy = prev2 * cw[0:1, :] + prev1 * cw[1:2, :] + xc * cw[2:3, :]
    cbcy_o[...] = (cb * cy).astype(BF16)


def _in_proj(x, w, conv_state, tm, n_groups, tk_store):
    m, d = x.shape
    sample_mode = conv_state is not None
    att_w = w["wq"].shape[1]
    gm_w = w["wu"].shape[1]
    group_len = m // n_groups
    tps = group_len // tm
    tks = min(tk_store, tm)
    full = lambda a: pl.BlockSpec(a.shape, lambda i: (0,) * a.ndim)
    rows = lambda width: pl.BlockSpec((tm, width), lambda i: (i, 0))
    ins = [x, w["norm_mix"], w["wq"], w["wkt"], w["wvt"], w["wu"], w["wgv"], w["wcb"], w["wcc"], w["wcx"],
           w["hs_att"], w["hs_gm"], w["q_gain"], w["k_gain_t"], w["gm_gain"], w["conv_w"]]
    in_specs = [rows(d)] + [full(a) for a in ins[1:]]
    if sample_mode:
        ins += list(conv_state)
        in_specs += [full(conv_state[0]), full(conv_state[1])]
    ft_f32 = jax.ShapeDtypeStruct((n_groups, att_w, group_len), F32)
    ft_bf16 = jax.ShapeDtypeStruct((n_groups, group_len // tks, att_w, tks), BF16)
    ft_f32_spec = pl.BlockSpec((None, att_w, tm), lambda i: (i // tps, 0, i % tps))
    ft_bf16_spec = pl.BlockSpec((None, tm // tks, att_w, tks), lambda i: (i // tps, i % tps, 0, 0))
    out_shape = [jax.ShapeDtypeStruct((m, att_w), BF16),
                 ft_f32, ft_bf16,
                 ft_f32, ft_bf16,
                 jax.ShapeDtypeStruct((m, gm_w), F32),
                 jax.ShapeDtypeStruct((m, gm_w), F32),
                 jax.ShapeDtypeStruct((m, gm_w), F32),
                 jax.ShapeDtypeStruct((m, gm_w), BF16)]
    out_specs = [rows(att_w), ft_f32_spec, ft_bf16_spec, ft_f32_spec, ft_bf16_spec,
                 rows(gm_w), rows(gm_w), rows(gm_w), rows(gm_w)]
    if sample_mode:
        scratch = [pltpu.VMEM((tm, gm_w), F32), pltpu.VMEM((tm, gm_w), F32)]
    else:
        scratch = [pltpu.VMEM((8, gm_w), F32)]
    return pl.pallas_call(
        functools.partial(_in_proj_kernel, sample_mode=sample_mode, tiles_per_seq=tps),
        grid=(m // tm,), in_specs=in_specs, out_specs=out_specs, out_shape=out_shape,
        scratch_shapes=scratch,
        compiler_params=pltpu.CompilerParams(dimension_semantics=("arbitrary",)),
        name="in_proj_sample" if sample_mode else "in_proj_prompt",
    )(*ins)


def _neg_log2_rest(z):
    neg_abs = lax.bitcast_convert_type(lax.bitcast_convert_type(z, jnp.uint32) | jnp.uint32(0x80000000), F32)
    return jnp.maximum(z, 0.0) + jnp.log(1.0 + jnp.exp2(neg_abs)) * (1.0 / LN2)


def _neg_suffix_ones(n):
    j = lax.broadcasted_iota(jnp.int32, (n, n), 0)
    s = lax.broadcasted_iota(jnp.int32, (n, n), 1)
    return jnp.where(j >= s, -1.0, 0.0).astype(BF16)


def _sb_weights(z, p_bf16, neg_suffix, carry):
    cs = _dot(p_bf16, neg_suffix)
    return jnp.exp2(z + cs + carry).astype(BF16), carry + cs[:, 0:1]


def _sb_prompt_kernel(bias_ref, q_ref, k_ref, v_ref, o_ref, acc_ref, carry_ref, z_ref, p_ref, a_ref, *, tq, tk):
    hp = pl.program_id(1)
    qb = pl.program_id(2)
    ratio = tq // tk
    n_stages = 4
    steady_start = max(ratio, n_stages - 1)
    assert 2 * ratio >= steady_start
    n_blocks = (qb + 1) * ratio
    q2 = q_ref[...]
    first = lax.broadcasted_iota(jnp.int32, (tq, LANES), 1) < HEAD_DIM
    qh = (jnp.where(first, q2, jnp.zeros_like(q2)), jnp.where(first, jnp.zeros_like(q2), q2))
    b2 = (bias_ref[2 * hp] * LOG2E, bias_ref[2 * hp + 1] * LOG2E)
    neg_suffix = _neg_suffix_ones(tk)
    acc_ref[...] = jnp.zeros_like(acc_ref)
    carry_ref[...] = jnp.zeros_like(carry_ref)

    def stage0(t):
        kblk = k_ref[n_blocks - 1 - t]
        for hh in range(2):
            z = _dot(qh[hh], kblk) + b2[hh]
            if isinstance(t, int) and t < ratio:
                trow = lax.broadcasted_iota(jnp.int32, (tq, tk), 0)
                scol = lax.broadcasted_iota(jnp.int32, (tq, tk), 1) + (ratio - 1 - t) * tk
                z = jnp.where(scol < trow, z, MASKED_LOG2)
            z_ref[t & 3, hh] = z

    def stage1(t):
        for hh in range(2):
            p_ref[t & 1, hh] = _neg_log2_rest(z_ref[t & 3, hh]).astype(BF16)

    def stage2(t):
        for hh in range(2):
            a_ref[t & 1, hh], carry_ref[hh] = _sb_weights(z_ref[t & 3, hh], p_ref[t & 1, hh], neg_suffix,
                                                          carry_ref[hh])

    def stage3(t):
        vblk = v_ref[n_blocks - 1 - t]
        pv = [_dot_nt(a_ref[t & 1, hh], vblk) for hh in range(2)]
        acc_ref[...] += jnp.where(first, pv[0], pv[1])

    stages = (stage0, stage1, stage2, stage3)

    def iteration(t, first_stage=0, last_stage=n_stages - 1):
        for s in reversed(range(first_stage, last_stage + 1)):
            stages[s](t - s)

    @pl.when(qb == 0)
    def _():
        for t in range(ratio):
            for stage in stages:
                stage(t)

    @pl.when(qb > 0)
    def _():
        for t in range(steady_start):
            iteration(t, last_stage=min(t, n_stages - 1))

        def body(t, c):
            iteration(t)
            return c
        lax.fori_loop(steady_start, n_blocks, body, 0)
        for i in range(1, n_stages):
            iteration(n_blocks - 1 + i, first_stage=i)

    o_ref[...] = acc_ref[...].astype(o_ref.dtype)


def _sb_prompt(q, kt, vt, bias, tq):
    m, att_w = q.shape
    n_seq, n_kblocks, _, tk = kt.shape
    seq = n_kblocks * tk
    n_pairs = att_w // LANES
    qblocks = seq // tq
    qspec = pl.BlockSpec((tq, LANES), lambda n, hp, qb: (n * qblocks + qb, hp))
    kvspec = pl.BlockSpec((None, n_kblocks, LANES, tk), lambda n, hp, qb: (n, 0, hp, 0))
    return pl.pallas_call(
        functools.partial(_sb_prompt_kernel, tq=tq, tk=tk),
        grid=(n_seq, n_pairs, qblocks),
        in_specs=[pl.BlockSpec(memory_space=pltpu.SMEM), qspec, kvspec, kvspec],
        out_specs=qspec,
        out_shape=jax.ShapeDtypeStruct((m, att_w), BF16),
        scratch_shapes=[pltpu.VMEM((tq, LANES), F32),
                        pltpu.VMEM((2, tq, 1), F32),
                        pltpu.VMEM((4, 2, tq, tk), F32),
                        pltpu.VMEM((2, 2, tq, tk), BF16),
                        pltpu.VMEM((2, 2, tq, tk), BF16)],
        compiler_params=pltpu.CompilerParams(dimension_semantics=("parallel", "parallel", "arbitrary")),
        name="sb_attention_prompt",
    )(bias, q, kt, vt)


def _sb_decode_kernel(pt_ref, bias_ref, qx_ref, kn_ref, vn_ref, *rest, n_pages):
    del pt_ref
    kpages = rest[:n_pages]
    vpages = rest[n_pages:2 * n_pages]
    o_ref = rest[2 * n_pages]
    qx = qx_ref[...]
    rows, att_w = qx.shape
    page = kpages[0].shape[1]
    dec_seq = o_ref.shape[0] // 2
    b2 = bias_ref[...] * LOG2E

    pad = jnp.zeros((page - kn_ref.shape[0], att_w), BF16)
    trow = lax.broadcasted_iota(jnp.int32, (rows, page), 0) & (dec_seq - 1)
    scol = lax.broadcasted_iota(jnp.int32, (rows, page), 1)
    z_new = jnp.where(scol < trow, _dot_nt(qx, jnp.concatenate([kn_ref[...], pad], axis=0)) + b2, MASKED_LOG2)
    z = jnp.concatenate([z_new] + [_dot(qx, kpages[i][...].astype(BF16)) + b2 for i in range(n_pages)], axis=0)
    cs = _dot(_neg_log2_rest(z).astype(BF16), _neg_suffix_ones(page))
    carries = [jnp.zeros((rows, 1), F32)]
    for i in range(n_pages):
        carries.append(carries[-1] + cs[i * rows:(i + 1) * rows, 0:1])
    a = jnp.exp2(z + cs + jnp.concatenate(carries, axis=0)).astype(BF16)
    acc = _dot(a[0:rows], jnp.concatenate([vn_ref[...], pad], axis=0))
    for i in range(n_pages):
        acc = acc + _dot_nt(a[(i + 1) * rows:(i + 2) * rows], vpages[i][...].astype(BF16))

    head_of_row = lax.broadcasted_iota(jnp.int32, acc.shape, 0) // dec_seq
    head_of_lane = lax.broadcasted_iota(jnp.int32, acc.shape, 1) // HEAD_DIM
    own = jnp.where(head_of_row == head_of_lane, acc, 0.0).astype(BF16)
    t_out = lax.broadcasted_iota(jnp.int32, (o_ref.shape[0], rows), 0)
    r_in = lax.broadcasted_iota(jnp.int32, (o_ref.shape[0], rows), 1)
    pick = jnp.where((r_in & (dec_seq - 1)) == t_out, 1.0, 0.0).astype(BF16)
    o_ref[...] = _dot(pick, own).astype(o_ref.dtype)


def _sb_decode(qx, k_new, v_new, bias_rows, cache_kt, cache_vt, layer, page_table):
    n_dec, rows, att_w = qx.shape
    new_rows = k_new.shape[1]
    page = cache_kt.shape[3]
    n_pages = page_table.shape[1]
    dec_seq = rows // (att_w // HEAD_DIM)

    def page_spec(i):
        return pl.BlockSpec((None, None, att_w, page), lambda b, pt: (layer, pt[b, n_pages - 1 - i], 0, 0))

    per_b = lambda r, c: pl.BlockSpec((None, r, c), lambda b, pt: (b, 0, 0))
    grid_spec = pltpu.PrefetchScalarGridSpec(
        num_scalar_prefetch=1,
        grid=(n_dec,),
        in_specs=([pl.BlockSpec(bias_rows.shape, lambda b, pt: (0, 0)), per_b(rows, att_w),
                   per_b(new_rows, att_w), per_b(new_rows, att_w)]
                  + [page_spec(i) for i in range(n_pages)] * 2),
        out_specs=per_b(2 * dec_seq, att_w),
    )
    return pl.pallas_call(
        functools.partial(_sb_decode_kernel, n_pages=n_pages),
        grid_spec=grid_spec,
        out_shape=jax.ShapeDtypeStruct((n_dec, 2 * dec_seq, att_w), BF16),
        compiler_params=pltpu.CompilerParams(dimension_semantics=("parallel",)),
        name="sb_attention_decode",
    )(page_table, bias_rows, qx, k_new, v_new, *([cache_kt] * n_pages), *([cache_vt] * n_pages))


def _merge_kernel(x_ref, nm_ref, wg_ref, att_ref, u_ref, gvn_ref, wsp_ref, bsp_ref, cbcy_ref,
                  wpa_ref, wpg_ref, wpc_ref, wout_ref, o_ref, *, sample_mode):
    x = x_ref[...]
    d = x.shape[1]
    xn = _rms_rows(x, nm_ref[...]).astype(BF16)
    tm = x.shape[0]
    gm_w = u_ref.shape[1]
    ug = []
    if sample_mode:
        n_tok = bsp_ref.shape[0]
        n_dec = tm // n_tok
        for t in range(n_tok):
            mixed = jnp.broadcast_to(bsp_ref[t:t + 1, :], (n_dec, gm_w))
            for s in range(t + 1):
                mixed = mixed + wsp_ref[t * n_tok + s:t * n_tok + s + 1, :] * gvn_ref[s * n_dec:(s + 1) * n_dec, :]
            ug.append((u_ref[t * n_dec:(t + 1) * n_dec, :] * mixed).astype(BF16))
    else:
        group_of_lane = lax.broadcasted_iota(jnp.int32, (CHUNK, gm_w), 1) // GM_GROUP_DIM
        for c in range(tm // CHUNK):
            sl = slice(c * CHUNK, (c + 1) * CHUNK)
            gv = gvn_ref[sl, :].astype(BF16)
            mixed = bsp_ref[...]
            for g in range(GM_GROUPS):
                mixed = mixed + jnp.where(group_of_lane == g, _dot(wsp_ref[g], gv), 0.0)
            ug.append((u_ref[sl, :] * mixed).astype(BF16))
    ug = jnp.concatenate(ug, axis=0)

    merged = jax.nn.sigmoid(_dot(xn, wg_ref[:, 0:d])) * _dot(att_ref[...], wpa_ref[...])
    merged = merged + jax.nn.sigmoid(_dot(xn, wg_ref[:, d:2 * d])) * _dot(ug, wpg_ref[...])
    merged = merged + jax.nn.sigmoid(_dot(xn, wg_ref[:, 2 * d:3 * d])) * _dot(cbcy_ref[...], wpc_ref[...])
    o_ref[...] = x + _dot(merged.astype(BF16), wout_ref[...])


def _merge(x, att, u, gvn, cbcy, w, wsp, bsp, tm, sample_mode):
    m, d = x.shape
    full = lambda a: pl.BlockSpec(a.shape, lambda i: (0,) * a.ndim)
    rows = lambda a: pl.BlockSpec((tm, a.shape[1]), lambda i: (i, 0))
    ins = [x, w["norm_mix"], w["wg"], att, u, gvn, wsp, bsp, cbcy, w["wpa"], w["wpg"], w["wpc"], w["wout"]]
    is_rows = [True, False, False, True, True, True, False, False, True, False, False, False, False]
    return pl.pallas_call(
        functools.partial(_merge_kernel, sample_mode=sample_mode), grid=(m // tm,),
        in_specs=[rows(a) if r else full(a) for a, r in zip(ins, is_rows)],
        out_specs=pl.BlockSpec((tm, d), lambda i: (i, 0)),
        out_shape=jax.ShapeDtypeStruct((m, d), F32),
        compiler_params=pltpu.CompilerParams(dimension_semantics=("parallel",)),
        name="merge_sample" if sample_mode else "merge_prompt",
    )(*ins)


def _mlp_kernel(x_ref, nm_ref, wup_ref, wdn_ref, o_ref):
    x = x_ref[...]
    xn = _rms_rows(x, nm_ref[...]).astype(BF16)
    acc = x
    for c in range(wup_ref.shape[1] // D_FF_CHUNK):
        sl = slice(c * D_FF_CHUNK, (c + 1) * D_FF_CHUNK)
        h = jnp.square(jnp.maximum(_dot(xn, wup_ref[:, sl]), 0.0)).astype(BF16)
        acc = acc + _dot(h, wdn_ref[sl, :])
    o_ref[...] = acc


def _mlp(x, w, tm):
    m, d = x.shape
    full = lambda a: pl.BlockSpec(a.shape, lambda i: (0,) * a.ndim)
    rows = pl.BlockSpec((tm, d), lambda i: (i, 0))
    return pl.pallas_call(
        _mlp_kernel, grid=(m // tm,),
        in_specs=[rows, full(w["norm_mlp"]), full(w["wup"]), full(w["wdn"])],
        out_specs=rows, out_shape=jax.ShapeDtypeStruct((m, d), F32),
        compiler_params=pltpu.CompilerParams(dimension_semantics=("parallel",)),
        name="mlp",
    )(x, w["norm_mlp"], w["wup"], w["wdn"])


def _blockdiag_ones(width, group):
    g = jnp.arange(width, dtype=jnp.int32) // group
    return (g[:, None] == g[None, :]).astype(BF16)


def _layer_weights(l, norm_mix, w_in, q_gain, k_gain, gm_gain, conv_w, w_proj_attn, w_proj_gmlp,
                   w_proj_conv, w_out, norm_mlp, w_up, w_down):
    d = w_in.shape[1]
    att_w = w_proj_attn.shape[1]
    gm_w = w_proj_gmlp.shape[1]
    conv_width = w_proj_conv.shape[1]
    n_heads = att_w // HEAD_DIM
    widths = [att_w] * 3 + [gm_w] * 2 + [conv_width] * 3 + [3 * d]
    offs = [0]
    for wd in widths:
        offs.append(offs[-1] + wd)
    wl = w_in[l].astype(BF16)
    names = ["wq", "wk", "wv", "wu", "wgv", "wcb", "wcc", "wcx", "wg"]
    w = {n: wl[:, offs[i]:offs[i + 1]] for i, n in enumerate(names)}
    w.update(
        wkt=w.pop("wk").T, wvt=w.pop("wv").T,
        norm_mix=norm_mix[l][None, :], norm_mlp=norm_mlp[l][None, :],
        q_gain=jnp.tile(q_gain[l], n_heads)[None, :],
        k_gain_t=jnp.broadcast_to(jnp.tile(k_gain[l], n_heads)[:, None], (att_w, LANES)),
        gm_gain=gm_gain[l][None, :], conv_w=conv_w[l],
        hs_att=_blockdiag_ones(att_w, HEAD_DIM), hs_gm=_blockdiag_ones(gm_w, GM_GROUP_DIM),
        wpa=w_proj_attn[l].astype(BF16), wpg=w_proj_gmlp[l].astype(BF16), wpc=w_proj_conv[l].astype(BF16),
        wout=w_out[l].astype(BF16), wup=w_up[l].astype(BF16), wdn=w_down[l].astype(BF16))
    return w


def _spatial_weights(w_spatial_l, b_spatial_l, dec_seq):
    r = jnp.arange(CHUNK, dtype=jnp.int32)
    tril = r[:, None] >= r[None, :]
    wsp_prompt = jnp.where(tril[None], w_spatial_l, 0.0).astype(BF16)
    bsp_prompt = jnp.repeat(b_spatial_l.T, GM_GROUP_DIM, axis=1)
    corner = w_spatial_l[:, :dec_seq, :dec_seq].reshape(GM_GROUPS, dec_seq * dec_seq)
    wsp_sample = jnp.repeat(corner.T, GM_GROUP_DIM, axis=1)
    bsp_sample = jnp.repeat(b_spatial_l[:, :dec_seq].T, GM_GROUP_DIM, axis=1)
    return wsp_prompt, bsp_prompt, wsp_sample, bsp_sample


def kernel(x_prompt, x_sample, cache_k, cache_v, state_conv, page_table, norm_mix, w_in, q_gain, k_gain, sb_bias, gm_gain, w_spatial, b_spatial, conv_w, w_proj_attn, w_proj_gmlp, w_proj_conv, w_out, norm_mlp, w_up, w_down):
    n_seq, seq, d = x_prompt.shape
    n_dec, dec_seq, _ = x_sample.shape
    depth, n_pool, page, n_heads, head_dim = cache_k.shape
    assert head_dim == HEAD_DIM and page == CHUNK and dec_seq == 4 and n_dec % 8 == 0
    att_w = n_heads * head_dim
    conv_width = state_conv.shape[-1]
    n_pages = page_table.shape[1]
    tm_prompt = min(ROW_TILE, seq)
    tq = min(Q_TILE, seq)
    assert seq % tm_prompt == 0 and seq % tq == 0 and tq % K_TILE == 0
    m_s = n_dec * dec_seq

    cache_kt = cache_k.transpose(0, 1, 3, 4, 2).reshape(depth, n_pool, att_w, page)
    cache_vt = cache_v.transpose(0, 1, 3, 4, 2).reshape(depth, n_pool, att_w, page)
    xp = x_prompt.reshape(n_seq * seq, d)
    xs = x_sample.transpose(1, 0, 2).reshape(m_s, d)
    head_of_lane = jnp.arange(att_w, dtype=jnp.int32) // HEAD_DIM
    own_head = head_of_lane[None, None, :] == jnp.arange(n_heads, dtype=jnp.int32)[None, :, None]

    outs = {name: [] for name in ("kp", "vp", "cp", "ks", "vs", "cs", "gs")}
    for l in range(depth):
        w = _layer_weights(l, norm_mix, w_in, q_gain, k_gain, gm_gain, conv_w, w_proj_attn,
                           w_proj_gmlp, w_proj_conv, w_out, norm_mlp, w_up, w_down)
        wsp_p, bsp_p, wsp_s, bsp_s = _spatial_weights(w_spatial[l], b_spatial[l], dec_seq)

        q, kf, kb, vf, vb, u, gvn, xc, cbcy = _in_proj(xp, w, None, tm_prompt, n_seq, K_TILE)
        att = _sb_prompt(q, kb, vb, sb_bias[l], tq)
        xp = _mlp(_merge(xp, att, u, gvn, cbcy, w, wsp_p, bsp_p, tm_prompt, False), w, tm_prompt)
        outs["kp"].append(kf)
        outs["vp"].append(vf)
        outs["cp"].append(xc.reshape(n_seq, seq, conv_width)[:, seq - (CONV_K - 1):])

        q, kf, kb, vf, vb, u, gvn, xc, cbcy = _in_proj(
            xs, w, (state_conv[l][:, 0], state_conv[l][:, 1]), n_dec, dec_seq, page)
        q3 = q.reshape(dec_seq, n_dec, att_w).transpose(1, 0, 2)
        qx = jnp.where(own_head[:, :, None, :], q3[:, None], jnp.zeros((), BF16))
        qx = qx.reshape(n_dec, n_heads * dec_seq, att_w)
        new_rows = lambda a: jnp.pad(a.reshape(dec_seq, att_w, n_dec).transpose(2, 0, 1),
                                     ((0, 0), (0, BF16_SUBLANES - dec_seq), (0, 0)))
        bias_rows = jnp.repeat(sb_bias[l], dec_seq)[:, None]
        att = _sb_decode(qx, new_rows(kb), new_rows(vb), bias_rows, cache_kt, cache_vt, l, page_table)
        att = att[:, :dec_seq].transpose(1, 0, 2).reshape(m_s, att_w)
        xs = _mlp(_merge(xs, att, u, gvn, cbcy, w, wsp_s, bsp_s, m_s, True), w, m_s)
        outs["ks"].append(kf)
        outs["vs"].append(vf)
        outs["cs"].append(xc.reshape(dec_seq, n_dec, conv_width)[dec_seq - (CONV_K - 1):].transpose(1, 0, 2))
        outs["gs"].append(gvn.reshape(dec_seq, n_dec, -1).transpose(1, 0, 2))

    stack = lambda name: jnp.stack(outs[name])
    prompt_kv = lambda name: stack(name).reshape(depth, n_seq, n_heads, head_dim, seq).transpose(0, 1, 4, 2, 3)
    sample_kv = lambda name: stack(name).reshape(depth, dec_seq, n_heads, head_dim, n_dec).transpose(0, 4, 1, 2, 3)
    return (xp.reshape(n_seq, seq, d), xs.reshape(dec_seq, n_dec, d).transpose(1, 0, 2),
            prompt_kv("kp"), prompt_kv("vp"), stack("cp"),
            sample_kv("ks"), sample_kv("vs"), stack("cs"), stack("gs"))
```

```python
import functools

import jax
import jax.numpy as jnp
from jax import lax
from jax.experimental import pallas as pl
from jax.experimental.pallas import tpu as pltpu

HEAD_DIM = 64
GM_GROUP_DIM = 64
GM_GROUPS = 4
CONV_K = 3
CHUNK = 128
EPS = 1e-6
LOG2E = 1.4426950408889634
LN2 = 0.6931471805599453
LANES = 128
ROW_TILE = 512
Q_TILE = 512
K_TILE = 256
BF16_SUBLANES = 16
D_FF_CHUNK = 1024
MASKED_LOG2 = -1e30
BF16 = jnp.bfloat16
F32 = jnp.float32


def _dot(a, b):
    return jnp.dot(a, b, preferred_element_type=F32)


def _dot_nt(a, b):
    return lax.dot_general(a, b, (((1,), (1,)), ((), ())), preferred_element_type=F32)


def _rms_rows(x, gain):
    ms = jnp.mean(x * x, axis=-1, keepdims=True)
    return x * lax.rsqrt(ms + EPS) * gain


def _group_rms_lanes(y, ones_blockdiag, group):
    sq = y * y
    hi = sq.astype(BF16)
    lo = (sq - hi.astype(F32)).astype(BF16)
    ss = _dot(hi, ones_blockdiag) + _dot(lo, ones_blockdiag)
    return y * lax.rsqrt(ss * (1.0 / group) + EPS)


def _group_rms_rows(yt, group):
    f, n = yt.shape
    y3 = yt.reshape(f // group, group, n)
    ss = jnp.sum(y3 * y3, axis=1, keepdims=True)
    return (y3 * lax.rsqrt(ss * (1.0 / group) + EPS)).reshape(f, n)


def _tile_lanes(a, n):
    return a[:, :n] if n <= a.shape[1] else jnp.concatenate([a] * (n // a.shape[1]), axis=1)


def _in_proj_kernel(*refs, sample_mode, tiles_per_seq):
    if sample_mode:
        (x_ref, nm_ref, wq, wkt, wvt, wu, wgv, wcb, wcc, wcx, hs_att, hs_gm, qg, kgt, gmg, cw,
         buf0_ref, buf1_ref, q_o, kf_o, kb_o, vf_o, vb_o, u_o, gvn_o, xc_o, cbcy_o, prev1_ref, prev2_ref) = refs
    else:
        (x_ref, nm_ref, wq, wkt, wvt, wu, wgv, wcb, wcc, wcx, hs_att, hs_gm, qg, kgt, gmg, cw, _, _,
         q_o, kf_o, kb_o, vf_o, vb_o, u_o, gvn_o, xc_o, cbcy_o, tail_ref) = refs

    xn = _rms_rows(x_ref[...], nm_ref[...]).astype(BF16)
    tm = xn.shape[0]
    tks = kb_o.shape[2]

    q = _group_rms_lanes(_dot(xn, wq[...]), hs_att[...], HEAD_DIM) * (qg[...] * (LOG2E * HEAD_DIM ** -0.5))
    q_o[...] = q.astype(BF16)
    kt = _group_rms_rows(_dot_nt(wkt[...], xn), HEAD_DIM) * _tile_lanes(kgt[...], tm)
    vt = _dot_nt(wvt[...], xn)
    kf_o[...] = kt
    vf_o[...] = vt
    for j in range(tm // tks):
        kb_o[j] = kt[:, j * tks:(j + 1) * tks].astype(BF16)
        vb_o[j] = vt[:, j * tks:(j + 1) * tks].astype(BF16)
    u_o[...] = _dot(xn, wu[...])
    gvn_o[...] = _group_rms_lanes(_dot(xn, wgv[...]), hs_gm[...], GM_GROUP_DIM) * gmg[...]

    cb = _dot(xn, wcb[...])
    xc = _dot(xn, wcc[...]) * _dot(xn, wcx[...])
    xc_o[...] = xc
    if sample_mode:
        @pl.when(pl.program_id(0) == 0)
        def _():
            prev2_ref[...] = buf0_ref[...]
            prev1_ref[...] = buf1_ref[...]
        prev1 = prev1_ref[...]
        prev2 = prev2_ref[...]
        prev2_ref[...] = prev1
        prev1_ref[...] = xc
    else:
        @pl.when(pl.program_id(0) % tiles_per_seq == 0)
        def _():
            tail_ref[...] = jnp.zeros_like(tail_ref)
        row = lax.broadcasted_iota(jnp.int32, xc.shape, 0)
        last2 = jnp.broadcast_to(tail_ref[0:1, :], xc.shape)
        last1 = jnp.broadcast_to(tail_ref[1:2, :], xc.shape)
        prev1 = jnp.where(row == 0, last1, pltpu.roll(xc, 1, axis=0))
        prev2 = jnp.where(row == 0, last2, jnp.where(row == 1, last1, pltpu.roll(xc, 2, axis=0)))
        tail_ref[0:2, :] = xc[tm - 2:tm, :]
    cy = prev2 * cw[0:1, :] + prev1 * cw[1:2, :] + xc * cw[2:3, :]
    cbcy_o[...] = (cb * cy).astype(BF16)


def _in_proj(x, w, conv_state, tm, n_groups, tk_store, kv_all=None):
    m, d = x.shape
    sample_mode = conv_state is not None
    assert sample_mode == (kv_all is None)
    att_w = w["wq"].shape[1]
    gm_w = w["wu"].shape[1]
    group_len = m // n_groups
    tps = group_len // tm
    tks = min(tk_store, tm)
    full = lambda a: pl.BlockSpec(a.shape, lambda i: (0,) * a.ndim)
    rows = lambda width: pl.BlockSpec((tm, width), lambda i: (i, 0))
    ins = [x, w["norm_mix"], w["wq"], w["wkt"], w["wvt"], w["wu"], w["wgv"], w["wcb"], w["wcc"], w["wcx"],
           w["hs_att"], w["hs_gm"], w["q_gain"], w["k_gain_t"], w["gm_gain"], w["conv_w"]]
    in_specs = [rows(d)] + [full(a) for a in ins[1:]]
    if sample_mode:
        ins += list(conv_state)
        in_specs += [full(conv_state[0]), full(conv_state[1])]
        aliases = {}
        ft_f32 = jax.ShapeDtypeStruct((n_groups, att_w, group_len), F32)
        ft_f32_spec = pl.BlockSpec((None, att_w, tm), lambda i: (i // tps, 0, i % tps))
    else:
        layer, k_all, v_all = kv_all
        aliases = {len(ins): 1, len(ins) + 1: 3}
        ins += [k_all, v_all]
        in_specs += [pl.BlockSpec(memory_space=pl.ANY)] * 2
        ft_f32 = jax.ShapeDtypeStruct(k_all.shape, F32)
        ft_f32_spec = pl.BlockSpec((None, None, att_w, tm), lambda i: (layer, i // tps, 0, i % tps))
    ft_bf16 = jax.ShapeDtypeStruct((n_groups, group_len // tks, att_w, tks), BF16)
    ft_bf16_spec = pl.BlockSpec((None, tm // tks, att_w, tks), lambda i: (i // tps, i % tps, 0, 0))
    out_shape = [jax.ShapeDtypeStruct((m, att_w), BF16),
                 ft_f32, ft_bf16,
                 ft_f32, ft_bf16,
                 jax.ShapeDtypeStruct((m, gm_w), F32),
                 jax.ShapeDtypeStruct((m, gm_w), F32),
                 jax.ShapeDtypeStruct((m, gm_w), F32),
                 jax.ShapeDtypeStruct((m, gm_w), BF16)]
    out_specs = [rows(att_w), ft_f32_spec, ft_bf16_spec, ft_f32_spec, ft_bf16_spec,
                 rows(gm_w), rows(gm_w), rows(gm_w), rows(gm_w)]
    if sample_mode:
        scratch = [pltpu.VMEM((tm, gm_w), F32), pltpu.VMEM((tm, gm_w), F32)]
    else:
        scratch = [pltpu.VMEM((8, gm_w), F32)]
    return pl.pallas_call(
        functools.partial(_in_proj_kernel, sample_mode=sample_mode, tiles_per_seq=tps),
        grid=(m // tm,), in_specs=in_specs, out_specs=out_specs, out_shape=out_shape,
        scratch_shapes=scratch, input_output_aliases=aliases,
        compiler_params=pltpu.CompilerParams(dimension_semantics=("arbitrary",)),
        name="in_proj_sample" if sample_mode else "in_proj_prompt",
    )(*ins)


def _neg_log2_rest(z):
    neg_abs = lax.bitcast_convert_type(lax.bitcast_convert_type(z, jnp.uint32) | jnp.uint32(0x80000000), F32)
    return jnp.maximum(z, 0.0) + jnp.log(1.0 + jnp.exp2(neg_abs)) * (1.0 / LN2)


def _neg_suffix_ones(n):
    j = lax.broadcasted_iota(jnp.int32, (n, n), 0)
    s = lax.broadcasted_iota(jnp.int32, (n, n), 1)
    return jnp.where(j >= s, -1.0, 0.0).astype(BF16)


def _sb_weights(z, p_bf16, neg_suffix, carry):
    cs = _dot(p_bf16, neg_suffix)
    return jnp.exp2(z + cs + carry).astype(BF16), carry + cs[:, 0:1]


def _sb_prompt_kernel(bias_ref, q_ref, k_ref, v_ref, o_ref, acc_ref, carry_ref, z_ref, p_ref, a_ref, *, tq, tk):
    hp = pl.program_id(1)
    qb = pl.program_id(2)
    ratio = tq // tk
    n_stages = 4
    steady_start = max(ratio, n_stages - 1)
    assert 2 * ratio >= steady_start
    n_blocks = (qb + 1) * ratio
    q2 = q_ref[...]
    first = lax.broadcasted_iota(jnp.int32, (tq, LANES), 1) < HEAD_DIM
    qh = (jnp.where(first, q2, jnp.zeros_like(q2)), jnp.where(first, jnp.zeros_like(q2), q2))
    b2 = (bias_ref[2 * hp] * LOG2E, bias_ref[2 * hp + 1] * LOG2E)
    neg_suffix = _neg_suffix_ones(tk)
    acc_ref[...] = jnp.zeros_like(acc_ref)
    carry_ref[...] = jnp.zeros_like(carry_ref)

    def live_rows(t):
        return slice((ratio - 1 - t) * tk if isinstance(t, int) and t < ratio else 0, tq)

    def stage0(t):
        kblk = k_ref[n_blocks - 1 - t]
        rows = live_rows(t)
        for hh in range(2):
            z = _dot(qh[hh][rows], kblk) + b2[hh]
            if isinstance(t, int) and t < ratio:
                trow = lax.broadcasted_iota(jnp.int32, z.shape, 0)
                scol = lax.broadcasted_iota(jnp.int32, z.shape, 1)
                z = jnp.where(scol < trow, z, MASKED_LOG2)
            z_ref[t & 3, hh, rows] = z

    def stage1(t):
        rows = live_rows(t)
        for hh in range(2):
            p_ref[t & 1, hh, rows] = _neg_log2_rest(z_ref[t & 3, hh, rows]).astype(BF16)

    def stage2(t):
        rows = live_rows(t)
        for hh in range(2):
            a_ref[t & 1, hh, rows], carry_ref[hh, rows] = _sb_weights(
                z_ref[t & 3, hh, rows], p_ref[t & 1, hh, rows], neg_suffix, carry_ref[hh, rows])
            if rows.start:
                a_ref[t & 1, hh, 0:rows.start] = jnp.zeros((rows.start, tk), BF16)

    def stage3(t):
        vblk = v_ref[n_blocks - 1 - t]
        pv = [_dot_nt(a_ref[t & 1, hh], vblk) for hh in range(2)]
        acc_ref[...] += jnp.where(first, pv[0], pv[1])

    stages = (stage0, stage1, stage2, stage3)

    def iteration(t, first_stage=0, last_stage=n_stages - 1):
        for s in reversed(range(first_stage, last_stage + 1)):
            stages[s](t - s)

    @pl.when(qb == 0)
    def _():
        for t in range(ratio):
            for stage in stages:
                stage(t)

    @pl.when(qb > 0)
    def _():
        for t in range(steady_start):
            iteration(t, last_stage=min(t, n_stages - 1))

        def body(t, c):
            iteration(t)
            return c
        lax.fori_loop(steady_start, n_blocks, body, 0)
        for i in range(1, n_stages):
            iteration(n_blocks - 1 + i, first_stage=i)

    o_ref[...] = acc_ref[...].astype(o_ref.dtype)


def _sb_prompt(q, kt, vt, bias, tq):
    m, att_w = q.shape
    n_seq, n_kblocks, _, tk = kt.shape
    seq = n_kblocks * tk
    n_pairs = att_w // LANES
    qblocks = seq // tq
    qspec = pl.BlockSpec((tq, LANES), lambda n, hp, qb: (n * qblocks + qb, hp))
    kvspec = pl.BlockSpec((None, n_kblocks, LANES, tk), lambda n, hp, qb: (n, 0, hp, 0))
    return pl.pallas_call(
        functools.partial(_sb_prompt_kernel, tq=tq, tk=tk),
        grid=(n_seq, n_pairs, qblocks),
        in_specs=[pl.BlockSpec(memory_space=pltpu.SMEM), qspec, kvspec, kvspec],
        out_specs=qspec,
        out_shape=jax.ShapeDtypeStruct((m, att_w), BF16),
        scratch_shapes=[pltpu.VMEM((tq, LANES), F32),
                        pltpu.VMEM((2, tq, 1), F32),
                        pltpu.VMEM((4, 2, tq, tk), F32),
                        pltpu.VMEM((2, 2, tq, tk), BF16),
                        pltpu.VMEM((2, 2, tq, tk), BF16)],
        compiler_params=pltpu.CompilerParams(dimension_semantics=("parallel", "parallel", "arbitrary")),
        name="sb_attention_prompt",
    )(bias, q, kt, vt)


def _sb_decode_kernel(pt_ref, bias_ref, qx_ref, kn_ref, vn_ref, *rest, n_pages):
    del pt_ref
    kpages = rest[:n_pages]
    vpages = rest[n_pages:2 * n_pages]
    o_ref = rest[2 * n_pages]
    qx = qx_ref[...]
    rows, att_w = qx.shape
    page = kpages[0].shape[1]
    dec_seq = o_ref.shape[0] // 2
    b2 = bias_ref[...] * LOG2E

    pad = jnp.zeros((page - kn_ref.shape[0], att_w), BF16)
    trow = lax.broadcasted_iota(jnp.int32, (rows, page), 0) & (dec_seq - 1)
    scol = lax.broadcasted_iota(jnp.int32, (rows, page), 1)
    z_new = jnp.where(scol < trow, _dot_nt(qx, jnp.concatenate([kn_ref[...], pad], axis=0)) + b2, MASKED_LOG2)
    z = jnp.concatenate([z_new] + [_dot(qx, kpages[i][...].astype(BF16)) + b2 for i in range(n_pages)], axis=0)
    cs = _dot(_neg_log2_rest(z).astype(BF16), _neg_suffix_ones(page))
    carries = [jnp.zeros((rows, 1), F32)]
    for i in range(n_pages):
        carries.append(carries[-1] + cs[i * rows:(i + 1) * rows, 0:1])
    a = jnp.exp2(z + cs + jnp.concatenate(carries, axis=0)).astype(BF16)
    acc = _dot(a[0:rows], jnp.concatenate([vn_ref[...], pad], axis=0))
    for i in range(n_pages):
        acc = acc + _dot_nt(a[(i + 1) * rows:(i + 2) * rows], vpages[i][...].astype(BF16))

    head_of_row = lax.broadcasted_iota(jnp.int32, acc.shape, 0) // dec_seq
    head_of_lane = lax.broadcasted_iota(jnp.int32, acc.shape, 1) // HEAD_DIM
    own = jnp.where(head_of_row == head_of_lane, acc, 0.0).astype(BF16)
    t_out = lax.broadcasted_iota(jnp.int32, (o_ref.shape[0], rows), 0)
    r_in = lax.broadcasted_iota(jnp.int32, (o_ref.shape[0], rows), 1)
    pick = jnp.where((r_in & (dec_seq - 1)) == t_out, 1.0, 0.0).astype(BF16)
    o_ref[...] = _dot(pick, own).astype(o_ref.dtype)


def _sb_decode(qx, k_new, v_new, bias_rows, cache_kt, cache_vt, layer, page_table):
    n_dec, rows, att_w = qx.shape
    new_rows = k_new.shape[1]
    page = cache_kt.shape[3]
    n_pages = page_table.shape[1]
    dec_seq = rows // (att_w // HEAD_DIM)

    def page_spec(i):
        return pl.BlockSpec((None, None, att_w, page), lambda b, pt: (layer, pt[b, n_pages - 1 - i], 0, 0))

    per_b = lambda r, c: pl.BlockSpec((None, r, c), lambda b, pt: (b, 0, 0))
    grid_spec = pltpu.PrefetchScalarGridSpec(
        num_scalar_prefetch=1,
        grid=(n_dec,),
        in_specs=([pl.BlockSpec(bias_rows.shape, lambda b, pt: (0, 0)), per_b(rows, att_w),
                   per_b(new_rows, att_w), per_b(new_rows, att_w)]
                  + [page_spec(i) for i in range(n_pages)] * 2),
        out_specs=per_b(2 * dec_seq, att_w),
    )
    return pl.pallas_call(
        functools.partial(_sb_decode_kernel, n_pages=n_pages),
        grid_spec=grid_spec,
        out_shape=jax.ShapeDtypeStruct((n_dec, 2 * dec_seq, att_w), BF16),
        compiler_params=pltpu.CompilerParams(dimension_semantics=("parallel",)),
        name="sb_attention_decode",
    )(page_table, bias_rows, qx, k_new, v_new, *([cache_kt] * n_pages), *([cache_vt] * n_pages))


def _merge_kernel(x_ref, nm_ref, wg_ref, att_ref, u_ref, gvn_ref, wsp_ref, bsp_ref, cbcy_ref,
                  wpa_ref, wpg_ref, wpc_ref, wout_ref, o_ref, *, sample_mode):
    x = x_ref[...]
    d = x.shape[1]
    xn = _rms_rows(x, nm_ref[...]).astype(BF16)
    tm = x.shape[0]
    gm_w = u_ref.shape[1]
    ug = []
    if sample_mode:
        n_tok = bsp_ref.shape[0]
        n_dec = tm // n_tok
        for t in range(n_tok):
            mixed = jnp.broadcast_to(bsp_ref[t:t + 1, :], (n_dec, gm_w))
            for s in range(t + 1):
                mixed = mixed + wsp_ref[t * n_tok + s:t * n_tok + s + 1, :] * gvn_ref[s * n_dec:(s + 1) * n_dec, :]
            ug.append((u_ref[t * n_dec:(t + 1) * n_dec, :] * mixed).astype(BF16))
    else:
        group_of_lane = lax.broadcasted_iota(jnp.int32, (CHUNK, gm_w), 1) // GM_GROUP_DIM
        for c in range(tm // CHUNK):
            sl = slice(c * CHUNK, (c + 1) * CHUNK)
            gv = gvn_ref[sl, :].astype(BF16)
            mixed = bsp_ref[...]
            for g in range(GM_GROUPS):
                mixed = mixed + jnp.where(group_of_lane == g, _dot(wsp_ref[g], gv), 0.0)
            ug.append((u_ref[sl, :] * mixed).astype(BF16))
    ug = jnp.concatenate(ug, axis=0)

    merged = jax.nn.sigmoid(_dot(xn, wg_ref[:, 0:d])) * _dot(att_ref[...], wpa_ref[...])
    merged = merged + jax.nn.sigmoid(_dot(xn, wg_ref[:, d:2 * d])) * _dot(ug, wpg_ref[...])
    merged = merged + jax.nn.sigmoid(_dot(xn, wg_ref[:, 2 * d:3 * d])) * _dot(cbcy_ref[...], wpc_ref[...])
    o_ref[...] = x + _dot(merged.astype(BF16), wout_ref[...])


def _merge(x, att, u, gvn, cbcy, w, wsp, bsp, tm, sample_mode):
    m, d = x.shape
    full = lambda a: pl.BlockSpec(a.shape, lambda i: (0,) * a.ndim)
    rows = lambda a: pl.BlockSpec((tm, a.shape[1]), lambda i: (i, 0))
    ins = [x, w["norm_mix"], w["wg"], att, u, gvn, wsp, bsp, cbcy, w["wpa"], w["wpg"], w["wpc"], w["wout"]]
    is_rows = [True, False, False, True, True, True, False, False, True, False, False, False, False]
    return pl.pallas_call(
        functools.partial(_merge_kernel, sample_mode=sample_mode), grid=(m // tm,),
        in_specs=[rows(a) if r else full(a) for a, r in zip(ins, is_rows)],
        out_specs=pl.BlockSpec((tm, d), lambda i: (i, 0)),
        out_shape=jax.ShapeDtypeStruct((m, d), F32),
        compiler_params=pltpu.CompilerParams(dimension_semantics=("parallel",)),
        name="merge_sample" if sample_mode else "merge_prompt",
    )(*ins)


def _mlp_kernel(x_ref, nm_ref, wup_ref, wdn_ref, o_ref):
    x = x_ref[...]
    xn = _rms_rows(x, nm_ref[...]).astype(BF16)
    acc = x
    for c in range(wup_ref.shape[1] // D_FF_CHUNK):
        sl = slice(c * D_FF_CHUNK, (c + 1) * D_FF_CHUNK)
        h = jnp.square(jnp.maximum(_dot(xn, wup_ref[:, sl]), 0.0)).astype(BF16)
        acc = acc + _dot(h, wdn_ref[sl, :])
    o_ref[...] = acc


def _mlp(x, w, tm):
    m, d = x.shape
    full = lambda a: pl.BlockSpec(a.shape, lambda i: (0,) * a.ndim)
    rows = pl.BlockSpec((tm, d), lambda i: (i, 0))
    return pl.pallas_call(
        _mlp_kernel, grid=(m // tm,),
        in_specs=[rows, full(w["norm_mlp"]), full(w["wup"]), full(w["wdn"])],
        out_specs=rows, out_shape=jax.ShapeDtypeStruct((m, d), F32),
        compiler_params=pltpu.CompilerParams(dimension_semantics=("parallel",)),
        name="mlp",
    )(x, w["norm_mlp"], w["wup"], w["wdn"])


def _blockdiag_ones(width, group):
    g = jnp.arange(width, dtype=jnp.int32) // group
    return (g[:, None] == g[None, :]).astype(BF16)


def _layer_weights(l, norm_mix, w_in, q_gain, k_gain, gm_gain, conv_w, w_proj_attn, w_proj_gmlp,
                   w_proj_conv, w_out, norm_mlp, w_up, w_down):
    d = w_in.shape[1]
    att_w = w_proj_attn.shape[1]
    gm_w = w_proj_gmlp.shape[1]
    conv_width = w_proj_conv.shape[1]
    n_heads = att_w // HEAD_DIM
    widths = [att_w] * 3 + [gm_w] * 2 + [conv_width] * 3 + [3 * d]
    offs = [0]
    for wd in widths:
        offs.append(offs[-1] + wd)
    wl = w_in[l].astype(BF16)
    names = ["wq", "wk", "wv", "wu", "wgv", "wcb", "wcc", "wcx", "wg"]
    w = {n: wl[:, offs[i]:offs[i + 1]] for i, n in enumerate(names)}
    w.update(
        wkt=w.pop("wk").T, wvt=w.pop("wv").T,
        norm_mix=norm_mix[l][None, :], norm_mlp=norm_mlp[l][None, :],
        q_gain=jnp.tile(q_gain[l], n_heads)[None, :],
        k_gain_t=jnp.broadcast_to(jnp.tile(k_gain[l], n_heads)[:, None], (att_w, LANES)),
        gm_gain=gm_gain[l][None, :], conv_w=conv_w[l],
        hs_att=_blockdiag_ones(att_w, HEAD_DIM), hs_gm=_blockdiag_ones(gm_w, GM_GROUP_DIM),
        wpa=w_proj_attn[l].astype(BF16), wpg=w_proj_gmlp[l].astype(BF16), wpc=w_proj_conv[l].astype(BF16),
        wout=w_out[l].astype(BF16), wup=w_up[l].astype(BF16), wdn=w_down[l].astype(BF16))
    return w


def _spatial_weights(w_spatial_l, b_spatial_l, dec_seq):
    r = jnp.arange(CHUNK, dtype=jnp.int32)
    tril = r[:, None] >= r[None, :]
    wsp_prompt = jnp.where(tril[None], w_spatial_l, 0.0).astype(BF16)
    bsp_prompt = jnp.repeat(b_spatial_l.T, GM_GROUP_DIM, axis=1)
    corner = w_spatial_l[:, :dec_seq, :dec_seq].reshape(GM_GROUPS, dec_seq * dec_seq)
    wsp_sample = jnp.repeat(corner.T, GM_GROUP_DIM, axis=1)
    bsp_sample = jnp.repeat(b_spatial_l[:, :dec_seq].T, GM_GROUP_DIM, axis=1)
    return wsp_prompt, bsp_prompt, wsp_sample, bsp_sample


def kernel(x_prompt, x_sample, cache_k, cache_v, state_conv, page_table, norm_mix, w_in, q_gain, k_gain, sb_bias, gm_gain, w_spatial, b_spatial, conv_w, w_proj_attn, w_proj_gmlp, w_proj_conv, w_out, norm_mlp, w_up, w_down):
    n_seq, seq, d = x_prompt.shape
    n_dec, dec_seq, _ = x_sample.shape
    depth, n_pool, page, n_heads, head_dim = cache_k.shape
    assert head_dim == HEAD_DIM and page == CHUNK and dec_seq == 4 and n_dec % 8 == 0
    att_w = n_heads * head_dim
    conv_width = state_conv.shape[-1]
    n_pages = page_table.shape[1]
    tm_prompt = min(ROW_TILE, seq)
    tq = min(Q_TILE, seq)
    assert seq % tm_prompt == 0 and seq % tq == 0 and tq % K_TILE == 0
    m_s = n_dec * dec_seq

    cache_kt = cache_k.transpose(0, 1, 3, 4, 2).reshape(depth, n_pool, att_w, page)
    cache_vt = cache_v.transpose(0, 1, 3, 4, 2).reshape(depth, n_pool, att_w, page)
    xp = x_prompt.reshape(n_seq * seq, d)
    xs = x_sample.transpose(1, 0, 2).reshape(m_s, d)
    head_of_lane = jnp.arange(att_w, dtype=jnp.int32) // HEAD_DIM
    own_head = head_of_lane[None, None, :] == jnp.arange(n_heads, dtype=jnp.int32)[None, :, None]

    outs = {name: [] for name in ("cp", "ks", "vs", "cs", "gs")}
    kp_all = jnp.zeros((depth, n_seq, att_w, seq), F32)
    vp_all = jnp.zeros((depth, n_seq, att_w, seq), F32)
    for l in range(depth):
        w = _layer_weights(l, norm_mix, w_in, q_gain, k_gain, gm_gain, conv_w, w_proj_attn,
                           w_proj_gmlp, w_proj_conv, w_out, norm_mlp, w_up, w_down)
        wsp_p, bsp_p, wsp_s, bsp_s = _spatial_weights(w_spatial[l], b_spatial[l], dec_seq)

        q, kp_all, kb, vp_all, vb, u, gvn, xc, cbcy = _in_proj(xp, w, None, tm_prompt, n_seq, K_TILE,
                                                               (l, kp_all, vp_all))
        att = _sb_prompt(q, kb, vb, sb_bias[l], tq)
        xp = _mlp(_merge(xp, att, u, gvn, cbcy, w, wsp_p, bsp_p, tm_prompt, False), w, tm_prompt)
        outs["cp"].append(xc.reshape(n_seq, seq, conv_width)[:, seq - (CONV_K - 1):])

        q, kf, kb, vf, vb, u, gvn, xc, cbcy = _in_proj(
            xs, w, (state_conv[l][:, 0], state_conv[l][:, 1]), n_dec, dec_seq, page)
        q3 = q.reshape(dec_seq, n_dec, att_w).transpose(1, 0, 2)
        qx = jnp.where(own_head[:, :, None, :], q3[:, None], jnp.zeros((), BF16))
        qx = qx.reshape(n_dec, n_heads * dec_seq, att_w)
        new_rows = lambda a: jnp.pad(a.reshape(dec_seq, att_w, n_dec).transpose(2, 0, 1),
                                     ((0, 0), (0, BF16_SUBLANES - dec_seq), (0, 0)))
        bias_rows = jnp.repeat(sb_bias[l], dec_seq)[:, None]
        att = _sb_decode(qx, new_rows(kb), new_rows(vb), bias_rows, cache_kt, cache_vt, l, page_table)
        att = att[:, :dec_seq].transpose(1, 0, 2).reshape(m_s, att_w)
        xs = _mlp(_merge(xs, att, u, gvn, cbcy, w, wsp_s, bsp_s, m_s, True), w, m_s)
        outs["ks"].append(kf)
        outs["vs"].append(vf)
        outs["cs"].append(xc.reshape(dec_seq, n_dec, conv_width)[dec_seq - (CONV_K - 1):].transpose(1, 0, 2))
        outs["gs"].append(gvn.reshape(dec_seq, n_dec, -1).transpose(1, 0, 2))

    stack = lambda name: jnp.stack(outs[name])
    prompt_kv = lambda a: a.reshape(depth, n_seq, n_heads, head_dim, seq).transpose(0, 1, 4, 2, 3)
    sample_kv = lambda name: stack(name).reshape(depth, dec_seq, n_heads, head_dim, n_dec).transpose(0, 4, 1, 2, 3)
    return (xp.reshape(n_seq, seq, d), xs.reshape(dec_seq, n_dec, d).transpose(1, 0, 2),
            prompt_kv(kp_all), prompt_kv(vp_all), stack("cp"),
            sample_kv("ks"), sample_kv("vs"), stack("cs"), stack("gs"))
```

```python
import functools

import jax
import jax.numpy as jnp
from jax import lax
from jax.experimental import pallas as pl
from jax.experimental.pallas import tpu as pltpu

HEAD_DIM = 64
GM_GROUP_DIM = 64
GM_GROUPS = 4
CONV_K = 3
CHUNK = 128
EPS = 1e-6
LOG2E = 1.4426950408889634
LN2 = 0.6931471805599453
LANES = 128
ROW_TILE = 512
Q_TILE = 512
K_TILE = 256
EARLY_P_NUM, EARLY_P_DEN = 3, 4
BF16_SUBLANES = 16
D_FF_CHUNK = 1024
MASKED_LOG2 = -1e30
BF16 = jnp.bfloat16
F32 = jnp.float32


def _dot(a, b):
    return jnp.dot(a, b, preferred_element_type=F32)


def _dot_nt(a, b):
    return lax.dot_general(a, b, (((1,), (1,)), ((), ())), preferred_element_type=F32)


def _rms_rows(x, gain):
    ms = jnp.mean(x * x, axis=-1, keepdims=True)
    return x * lax.rsqrt(ms + EPS) * gain


def _group_rms_lanes(y, ones_blockdiag, group):
    sq = y * y
    hi = sq.astype(BF16)
    lo = (sq - hi.astype(F32)).astype(BF16)
    ss = _dot(hi, ones_blockdiag) + _dot(lo, ones_blockdiag)
    return y * lax.rsqrt(ss * (1.0 / group) + EPS)


def _group_rms_rows(yt, group):
    f, n = yt.shape
    y3 = yt.reshape(f // group, group, n)
    ss = jnp.sum(y3 * y3, axis=1, keepdims=True)
    return (y3 * lax.rsqrt(ss * (1.0 / group) + EPS)).reshape(f, n)


def _tile_lanes(a, n):
    return a[:, :n] if n <= a.shape[1] else jnp.concatenate([a] * (n // a.shape[1]), axis=1)


def _in_proj_kernel(*refs, sample_mode, tiles_per_seq):
    if sample_mode:
        (x_ref, nm_ref, wq, wkt, wvt, wu, wgv, wcb, wcc, wcx, hs_att, hs_gm, qg, kgt, gmg, cw,
         buf0_ref, buf1_ref, q_o, kf_o, kb_o, vf_o, vb_o, u_o, gvn_o, xc_o, cbcy_o, prev1_ref, prev2_ref) = refs
    else:
        (x_ref, nm_ref, wq, wkt, wvt, wu, wgv, wcb, wcc, wcx, hs_att, hs_gm, qg, kgt, gmg, cw, _, _,
         q_o, kf_o, kb_o, vf_o, vb_o, u_o, gvn_o, xc_o, cbcy_o, tail_ref) = refs

    xn = _rms_rows(x_ref[...], nm_ref[...]).astype(BF16)
    tm = xn.shape[0]
    tks = kb_o.shape[2]

    q = _group_rms_lanes(_dot(xn, wq[...]), hs_att[...], HEAD_DIM) * (qg[...] * (LOG2E * HEAD_DIM ** -0.5))
    q_o[...] = q.astype(BF16)
    kt = _group_rms_rows(_dot_nt(wkt[...], xn), HEAD_DIM) * _tile_lanes(kgt[...], tm)
    vt = _dot_nt(wvt[...], xn)
    kf_o[...] = kt
    vf_o[...] = vt
    for j in range(tm // tks):
        kb_o[j] = kt[:, j * tks:(j + 1) * tks].astype(BF16)
        vb_o[j] = vt[:, j * tks:(j + 1) * tks].astype(BF16)
    u_o[...] = _dot(xn, wu[...])
    gvn_o[...] = _group_rms_lanes(_dot(xn, wgv[...]), hs_gm[...], GM_GROUP_DIM) * gmg[...]

    cb = _dot(xn, wcb[...])
    xc = _dot(xn, wcc[...]) * _dot(xn, wcx[...])
    xc_o[...] = xc
    if sample_mode:
        @pl.when(pl.program_id(0) == 0)
        def _():
            prev2_ref[...] = buf0_ref[...]
            prev1_ref[...] = buf1_ref[...]
        prev1 = prev1_ref[...]
        prev2 = prev2_ref[...]
        prev2_ref[...] = prev1
        prev1_ref[...] = xc
    else:
        @pl.when(pl.program_id(0) % tiles_per_seq == 0)
        def _():
            tail_ref[...] = jnp.zeros_like(tail_ref)
        row = lax.broadcasted_iota(jnp.int32, xc.shape, 0)
        last2 = jnp.broadcast_to(tail_ref[0:1, :], xc.shape)
        last1 = jnp.broadcast_to(tail_ref[1:2, :], xc.shape)
        prev1 = jnp.where(row == 0, last1, pltpu.roll(xc, 1, axis=0))
        prev2 = jnp.where(row == 0, last2, jnp.where(row == 1, last1, pltpu.roll(xc, 2, axis=0)))
        tail_ref[0:2, :] = xc[tm - 2:tm, :]
    cy = prev2 * cw[0:1, :] + prev1 * cw[1:2, :] + xc * cw[2:3, :]
    cbcy_o[...] = (cb * cy).astype(BF16)


def _in_proj(x, w, conv_state, tm, n_groups, tk_store, kv_all=None):
    m, d = x.shape
    sample_mode = conv_state is not None
    assert sample_mode == (kv_all is None)
    att_w = w["wq"].shape[1]
    gm_w = w["wu"].shape[1]
    group_len = m // n_groups
    tps = group_len // tm
    tks = min(tk_store, tm)
    full = lambda a: pl.BlockSpec(a.shape, lambda i: (0,) * a.ndim)
    rows = lambda width: pl.BlockSpec((tm, width), lambda i: (i, 0))
    ins = [x, w["norm_mix"], w["wq"], w["wkt"], w["wvt"], w["wu"], w["wgv"], w["wcb"], w["wcc"], w["wcx"],
           w["hs_att"], w["hs_gm"], w["q_gain"], w["k_gain_t"], w["gm_gain"], w["conv_w"]]
    in_specs = [rows(d)] + [full(a) for a in ins[1:]]
    if sample_mode:
        ins += list(conv_state)
        in_specs += [full(conv_state[0]), full(conv_state[1])]
        aliases = {}
        ft_f32 = jax.ShapeDtypeStruct((n_groups, att_w, group_len), F32)
        ft_f32_spec = pl.BlockSpec((None, att_w, tm), lambda i: (i // tps, 0, i % tps))
    else:
        layer, k_all, v_all = kv_all
        aliases = {len(ins): 1, len(ins) + 1: 3}
        ins += [k_all, v_all]
        in_specs += [pl.BlockSpec(memory_space=pl.ANY)] * 2
        ft_f32 = jax.ShapeDtypeStruct(k_all.shape, F32)
        ft_f32_spec = pl.BlockSpec((None, None, att_w, tm), lambda i: (layer, i // tps, 0, i % tps))
    ft_bf16 = jax.ShapeDtypeStruct((n_groups, group_len // tks, att_w, tks), BF16)
    ft_bf16_spec = pl.BlockSpec((None, tm // tks, att_w, tks), lambda i: (i // tps, i % tps, 0, 0))
    out_shape = [jax.ShapeDtypeStruct((m, att_w), BF16),
                 ft_f32, ft_bf16,
                 ft_f32, ft_bf16,
                 jax.ShapeDtypeStruct((m, gm_w), F32),
                 jax.ShapeDtypeStruct((m, gm_w), F32),
                 jax.ShapeDtypeStruct((m, gm_w), F32),
                 jax.ShapeDtypeStruct((m, gm_w), BF16)]
    out_specs = [rows(att_w), ft_f32_spec, ft_bf16_spec, ft_f32_spec, ft_bf16_spec,
                 rows(gm_w), rows(gm_w), rows(gm_w), rows(gm_w)]
    if sample_mode:
        scratch = [pltpu.VMEM((tm, gm_w), F32), pltpu.VMEM((tm, gm_w), F32)]
    else:
        scratch = [pltpu.VMEM((8, gm_w), F32)]
    return pl.pallas_call(
        functools.partial(_in_proj_kernel, sample_mode=sample_mode, tiles_per_seq=tps),
        grid=(m // tm,), in_specs=in_specs, out_specs=out_specs, out_shape=out_shape,
        scratch_shapes=scratch, input_output_aliases=aliases,
        compiler_params=pltpu.CompilerParams(dimension_semantics=("arbitrary",)),
        name="in_proj_sample" if sample_mode else "in_proj_prompt",
    )(*ins)


def _neg_log2_rest(z):
    return jnp.maximum(z, 0.0) + jnp.log(1.0 + jnp.exp2(-jnp.abs(z))) * (1.0 / LN2)


def _neg_suffix_ones(n):
    j = lax.broadcasted_iota(jnp.int32, (n, n), 0)
    s = lax.broadcasted_iota(jnp.int32, (n, n), 1)
    return jnp.where(j >= s, -1.0, 0.0).astype(BF16)


def _sb_weights(z, p_bf16, neg_suffix, carry):
    cs = _dot(p_bf16, neg_suffix)
    return jnp.exp2(z + cs + carry).astype(BF16), carry + cs[:, 0:1]


def _sb_prompt_kernel(bias_ref, q_ref, k_ref, v_ref, o_ref, acc_ref, carry_ref, z_ref, p_ref, a_ref,
                      *, tq, tk, early_rows):
    hp = pl.program_id(1)
    qb = pl.program_id(2)
    ratio = tq // tk
    n_stages = 4
    steady_start = max(ratio, n_stages - 1)
    assert 2 * ratio >= steady_start
    n_blocks = (qb + 1) * ratio
    q2 = q_ref[...]
    first = lax.broadcasted_iota(jnp.int32, (tq, LANES), 1) < HEAD_DIM
    qh = (jnp.where(first, q2, jnp.zeros_like(q2)), jnp.where(first, jnp.zeros_like(q2), q2))
    b2 = (bias_ref[2 * hp] * LOG2E, bias_ref[2 * hp + 1] * LOG2E)
    neg_suffix = _neg_suffix_ones(tk)
    acc_ref[...] = jnp.zeros_like(acc_ref)
    carry_ref[...] = jnp.zeros_like(carry_ref)

    def live_rows(t):
        return slice((ratio - 1 - t) * tk if isinstance(t, int) and t < ratio else 0, tq)

    def stage0(t):
        kblk = k_ref[n_blocks - 1 - t]
        rows = live_rows(t)
        for hh in range(2):
            z = _dot(qh[hh][rows], kblk) + b2[hh]
            if isinstance(t, int) and t < ratio:
                trow = lax.broadcasted_iota(jnp.int32, z.shape, 0)
                scol = lax.broadcasted_iota(jnp.int32, z.shape, 1)
                z = jnp.where(scol < trow, z, MASKED_LOG2)
            z_ref[t & 3, hh, rows] = z
            if rows.start < early_rows:
                p_ref[t & 1, hh, rows.start:early_rows] = _neg_log2_rest(z[:early_rows - rows.start]).astype(BF16)

    def stage1(t):
        rows = live_rows(t)
        rows = slice(max(rows.start, early_rows), tq)
        for hh in range(2):
            p_ref[t & 1, hh, rows] = _neg_log2_rest(z_ref[t & 3, hh, rows]).astype(BF16)

    def stage2(t):
        rows = live_rows(t)
        for hh in range(2):
            a_ref[t & 1, hh, rows], carry_ref[hh, rows] = _sb_weights(
                z_ref[t & 3, hh, rows], p_ref[t & 1, hh, rows], neg_suffix, carry_ref[hh, rows])
            if rows.start:
                a_ref[t & 1, hh, 0:rows.start] = jnp.zeros((rows.start, tk), BF16)

    def stage3(t):
        vblk = v_ref[n_blocks - 1 - t]
        pv = [_dot_nt(a_ref[t & 1, hh], vblk) for hh in range(2)]
        acc_ref[...] += jnp.where(first, pv[0], pv[1])

    stages = (stage0, stage1, stage2, stage3)

    def iteration(t, first_stage=0, last_stage=n_stages - 1):
        for s in reversed(range(first_stage, last_stage + 1)):
            stages[s](t - s)

    @pl.when(qb == 0)
    def _():
        for t in range(ratio):
            for stage in stages:
                stage(t)

    @pl.when(qb > 0)
    def _():
        for t in range(steady_start):
            iteration(t, last_stage=min(t, n_stages - 1))

        def body(t, c):
            iteration(t)
            return c
        lax.fori_loop(steady_start, n_blocks, body, 0)
        for i in range(1, n_stages):
            iteration(n_blocks - 1 + i, first_stage=i)

    o_ref[...] = acc_ref[...].astype(o_ref.dtype)


def _sb_prompt(q, kt, vt, bias, tq):
    m, att_w = q.shape
    n_seq, n_kblocks, _, tk = kt.shape
    seq = n_kblocks * tk
    n_pairs = att_w // LANES
    qblocks = seq // tq
    qspec = pl.BlockSpec((tq, LANES), lambda n, hp, qb: (n * qblocks + qb, hp))
    kvspec = pl.BlockSpec((None, n_kblocks, LANES, tk), lambda n, hp, qb: (n, 0, hp, 0))
    return pl.pallas_call(
        functools.partial(_sb_prompt_kernel, tq=tq, tk=tk, early_rows=tq * EARLY_P_NUM // EARLY_P_DEN),
        grid=(n_seq, n_pairs, qblocks),
        in_specs=[pl.BlockSpec(memory_space=pltpu.SMEM), qspec, kvspec, kvspec],
        out_specs=qspec,
        out_shape=jax.ShapeDtypeStruct((m, att_w), BF16),
        scratch_shapes=[pltpu.VMEM((tq, LANES), F32),
                        pltpu.VMEM((2, tq, 1), F32),
                        pltpu.VMEM((4, 2, tq, tk), F32),
                        pltpu.VMEM((2, 2, tq, tk), BF16),
                        pltpu.VMEM((2, 2, tq, tk), BF16)],
        compiler_params=pltpu.CompilerParams(dimension_semantics=("parallel", "parallel", "arbitrary")),
        name="sb_attention_prompt",
    )(bias, q, kt, vt)


def _sb_decode_kernel(pt_ref, bias_ref, qx_ref, kn_ref, vn_ref, *rest, n_pages):
    del pt_ref
    kpages = rest[:n_pages]
    vpages = rest[n_pages:2 * n_pages]
    o_ref = rest[2 * n_pages]
    qx = qx_ref[...]
    rows, att_w = qx.shape
    page = kpages[0].shape[1]
    dec_seq = o_ref.shape[0] // 2
    b2 = bias_ref[...] * LOG2E

    pad = jnp.zeros((page - kn_ref.shape[0], att_w), BF16)
    trow = lax.broadcasted_iota(jnp.int32, (rows, page), 0) & (dec_seq - 1)
    scol = lax.broadcasted_iota(jnp.int32, (rows, page), 1)
    z_new = jnp.where(scol < trow, _dot_nt(qx, jnp.concatenate([kn_ref[...], pad], axis=0)) + b2, MASKED_LOG2)
    z = jnp.concatenate([z_new] + [_dot(qx, kpages[i][...].astype(BF16)) + b2 for i in range(n_pages)], axis=0)
    cs = _dot(_neg_log2_rest(z).astype(BF16), _neg_suffix_ones(page))
    carries = [jnp.zeros((rows, 1), F32)]
    for i in range(n_pages):
        carries.append(carries[-1] + cs[i * rows:(i + 1) * rows, 0:1])
    a = jnp.exp2(z + cs + jnp.concatenate(carries, axis=0)).astype(BF16)
    acc = _dot(a[0:rows], jnp.concatenate([vn_ref[...], pad], axis=0))
    for i in range(n_pages):
        acc = acc + _dot_nt(a[(i + 1) * rows:(i + 2) * rows], vpages[i][...].astype(BF16))

    head_of_row = lax.broadcasted_iota(jnp.int32, acc.shape, 0) // dec_seq
    head_of_lane = lax.broadcasted_iota(jnp.int32, acc.shape, 1) // HEAD_DIM
    own = jnp.where(head_of_row == head_of_lane, acc, 0.0).astype(BF16)
    t_out = lax.broadcasted_iota(jnp.int32, (o_ref.shape[0], rows), 0)
    r_in = lax.broadcasted_iota(jnp.int32, (o_ref.shape[0], rows), 1)
    pick = jnp.where((r_in & (dec_seq - 1)) == t_out, 1.0, 0.0).astype(BF16)
    o_ref[...] = _dot(pick, own).astype(o_ref.dtype)


def _sb_decode(qx, k_new, v_new, bias_rows, cache_kt, cache_vt, layer, page_table):
    n_dec, rows, att_w = qx.shape
    new_rows = k_new.shape[1]
    page = cache_kt.shape[3]
    n_pages = page_table.shape[1]
    dec_seq = rows // (att_w // HEAD_DIM)

    def page_spec(i):
        return pl.BlockSpec((None, None, att_w, page), lambda b, pt: (layer, pt[b, n_pages - 1 - i], 0, 0))

    per_b = lambda r, c: pl.BlockSpec((None, r, c), lambda b, pt: (b, 0, 0))
    grid_spec = pltpu.PrefetchScalarGridSpec(
        num_scalar_prefetch=1,
        grid=(n_dec,),
        in_specs=([pl.BlockSpec(bias_rows.shape, lambda b, pt: (0, 0)), per_b(rows, att_w),
                   per_b(new_rows, att_w), per_b(new_rows, att_w)]
                  + [page_spec(i) for i in range(n_pages)] * 2),
        out_specs=per_b(2 * dec_seq, att_w),
    )
    return pl.pallas_call(
        functools.partial(_sb_decode_kernel, n_pages=n_pages),
        grid_spec=grid_spec,
        out_shape=jax.ShapeDtypeStruct((n_dec, 2 * dec_seq, att_w), BF16),
        compiler_params=pltpu.CompilerParams(dimension_semantics=("parallel",)),
        name="sb_attention_decode",
    )(page_table, bias_rows, qx, k_new, v_new, *([cache_kt] * n_pages), *([cache_vt] * n_pages))


def _merge_kernel(x_ref, nm_ref, wg_ref, att_ref, u_ref, gvn_ref, wsp_ref, bsp_ref, cbcy_ref,
                  wpa_ref, wpg_ref, wpc_ref, wout_ref, o_ref, *, sample_mode):
    x = x_ref[...]
    d = x.shape[1]
    xn = _rms_rows(x, nm_ref[...]).astype(BF16)
    tm = x.shape[0]
    gm_w = u_ref.shape[1]
    ug = []
    if sample_mode:
        n_tok = bsp_ref.shape[0]
        n_dec = tm // n_tok
        for t in range(n_tok):
            mixed = jnp.broadcast_to(bsp_ref[t:t + 1, :], (n_dec, gm_w))
            for s in range(t + 1):
                mixed = mixed + wsp_ref[t * n_tok + s:t * n_tok + s + 1, :] * gvn_ref[s * n_dec:(s + 1) * n_dec, :]
            ug.append((u_ref[t * n_dec:(t + 1) * n_dec, :] * mixed).astype(BF16))
    else:
        group_of_lane = lax.broadcasted_iota(jnp.int32, (CHUNK, gm_w), 1) // GM_GROUP_DIM
        for c in range(tm // CHUNK):
            sl = slice(c * CHUNK, (c + 1) * CHUNK)
            gv = gvn_ref[sl, :].astype(BF16)
            mixed = bsp_ref[...]
            for g in range(GM_GROUPS):
                mixed = mixed + jnp.where(group_of_lane == g, _dot(wsp_ref[g], gv), 0.0)
            ug.append((u_ref[sl, :] * mixed).astype(BF16))
    ug = jnp.concatenate(ug, axis=0)

    merged = jax.nn.sigmoid(_dot(xn, wg_ref[:, 0:d])) * _dot(att_ref[...], wpa_ref[...])
    merged = merged + jax.nn.sigmoid(_dot(xn, wg_ref[:, d:2 * d])) * _dot(ug, wpg_ref[...])
    merged = merged + jax.nn.sigmoid(_dot(xn, wg_ref[:, 2 * d:3 * d])) * _dot(cbcy_ref[...], wpc_ref[...])
    o_ref[...] = x + _dot(merged.astype(BF16), wout_ref[...])


def _merge(x, att, u, gvn, cbcy, w, wsp, bsp, tm, sample_mode):
    m, d = x.shape
    full = lambda a: pl.BlockSpec(a.shape, lambda i: (0,) * a.ndim)
    rows = lambda a: pl.BlockSpec((tm, a.shape[1]), lambda i: (i, 0))
    ins = [x, w["norm_mix"], w["wg"], att, u, gvn, wsp, bsp, cbcy, w["wpa"], w["wpg"], w["wpc"], w["wout"]]
    is_rows = [True, False, False, True, True, True, False, False, True, False, False, False, False]
    return pl.pallas_call(
        functools.partial(_merge_kernel, sample_mode=sample_mode), grid=(m // tm,),
        in_specs=[rows(a) if r else full(a) for a, r in zip(ins, is_rows)],
        out_specs=pl.BlockSpec((tm, d), lambda i: (i, 0)),
        out_shape=jax.ShapeDtypeStruct((m, d), F32),
        compiler_params=pltpu.CompilerParams(dimension_semantics=("parallel",)),
        name="merge_sample" if sample_mode else "merge_prompt",
    )(*ins)


def _mlp_kernel(x_ref, nm_ref, wup_ref, wdn_ref, o_ref):
    x = x_ref[...]
    xn = _rms_rows(x, nm_ref[...]).astype(BF16)
    acc = x
    for c in range(wup_ref.shape[1] // D_FF_CHUNK):
        sl = slice(c * D_FF_CHUNK, (c + 1) * D_FF_CHUNK)
        h = jnp.square(jnp.maximum(_dot(xn, wup_ref[:, sl]), 0.0)).astype(BF16)
        acc = acc + _dot(h, wdn_ref[sl, :])
    o_ref[...] = acc


def _mlp(x, w, tm):
    m, d = x.shape
    full = lambda a: pl.BlockSpec(a.shape, lambda i: (0,) * a.ndim)
    rows = pl.BlockSpec((tm, d), lambda i: (i, 0))
    return pl.pallas_call(
        _mlp_kernel, grid=(m // tm,),
        in_specs=[rows, full(w["norm_mlp"]), full(w["wup"]), full(w["wdn"])],
        out_specs=rows, out_shape=jax.ShapeDtypeStruct((m, d), F32),
        compiler_params=pltpu.CompilerParams(dimension_semantics=("parallel",)),
        name="mlp",
    )(x, w["norm_mlp"], w["wup"], w["wdn"])


def _blockdiag_ones(width, group):
    g = jnp.arange(width, dtype=jnp.int32) // group
    return (g[:, None] == g[None, :]).astype(BF16)


def _layer_weights(l, norm_mix, w_in, q_gain, k_gain, gm_gain, conv_w, w_proj_attn, w_proj_gmlp,
                   w_proj_conv, w_out, norm_mlp, w_up, w_down):
    d = w_in.shape[1]
    att_w = w_proj_attn.shape[1]
    gm_w = w_proj_gmlp.shape[1]
    conv_width = w_proj_conv.shape[1]
    n_heads = att_w // HEAD_DIM
    widths = [att_w] * 3 + [gm_w] * 2 + [conv_width] * 3 + [3 * d]
    offs = [0]
    for wd in widths:
        offs.append(offs[-1] + wd)
    wl = w_in[l].astype(BF16)
    names = ["wq", "wk", "wv", "wu", "wgv", "wcb", "wcc", "wcx", "wg"]
    w = {n: wl[:, offs[i]:offs[i + 1]] for i, n in enumerate(names)}
    w.update(
        wkt=w.pop("wk").T, wvt=w.pop("wv").T,
        norm_mix=norm_mix[l][None, :], norm_mlp=norm_mlp[l][None, :],
        q_gain=jnp.tile(q_gain[l], n_heads)[None, :],
        k_gain_t=jnp.broadcast_to(jnp.tile(k_gain[l], n_heads)[:, None], (att_w, LANES)),
        gm_gain=gm_gain[l][None, :], conv_w=conv_w[l],
        hs_att=_blockdiag_ones(att_w, HEAD_DIM), hs_gm=_blockdiag_ones(gm_w, GM_GROUP_DIM),
        wpa=w_proj_attn[l].astype(BF16), wpg=w_proj_gmlp[l].astype(BF16), wpc=w_proj_conv[l].astype(BF16),
        wout=w_out[l].astype(BF16), wup=w_up[l].astype(BF16), wdn=w_down[l].astype(BF16))
    return w


def _spatial_weights(w_spatial_l, b_spatial_l, dec_seq):
    r = jnp.arange(CHUNK, dtype=jnp.int32)
    tril = r[:, None] >= r[None, :]
    wsp_prompt = jnp.where(tril[None], w_spatial_l, 0.0).astype(BF16)
    bsp_prompt = jnp.repeat(b_spatial_l.T, GM_GROUP_DIM, axis=1)
    corner = w_spatial_l[:, :dec_seq, :dec_seq].reshape(GM_GROUPS, dec_seq * dec_seq)
    wsp_sample = jnp.repeat(corner.T, GM_GROUP_DIM, axis=1)
    bsp_sample = jnp.repeat(b_spatial_l[:, :dec_seq].T, GM_GROUP_DIM, axis=1)
    return wsp_prompt, bsp_prompt, wsp_sample, bsp_sample


def kernel(x_prompt, x_sample, cache_k, cache_v, state_conv, page_table, norm_mix, w_in, q_gain, k_gain, sb_bias, gm_gain, w_spatial, b_spatial, conv_w, w_proj_attn, w_proj_gmlp, w_proj_conv, w_out, norm_mlp, w_up, w_down):
    n_seq, seq, d = x_prompt.shape
    n_dec, dec_seq, _ = x_sample.shape
    depth, n_pool, page, n_heads, head_dim = cache_k.shape
    assert head_dim == HEAD_DIM and page == CHUNK and dec_seq == 4 and n_dec % 8 == 0
    att_w = n_heads * head_dim
    conv_width = state_conv.shape[-1]
    n_pages = page_table.shape[1]
    tm_prompt = min(ROW_TILE, seq)
    tq = min(Q_TILE, seq)
    assert seq % tm_prompt == 0 and seq % tq == 0 and tq % K_TILE == 0
    m_s = n_dec * dec_seq

    cache_kt = cache_k.transpose(0, 1, 3, 4, 2).reshape(depth, n_pool, att_w, page)
    cache_vt = cache_v.transpose(0, 1, 3, 4, 2).reshape(depth, n_pool, att_w, page)
    xp = x_prompt.reshape(n_seq * seq, d)
    xs = x_sample.transpose(1, 0, 2).reshape(m_s, d)
    head_of_lane = jnp.arange(att_w, dtype=jnp.int32) // HEAD_DIM
    own_head = head_of_lane[None, None, :] == jnp.arange(n_heads, dtype=jnp.int32)[None, :, None]

    outs = {name: [] for name in ("cp", "ks", "vs", "cs", "gs")}
    kp_all = jnp.zeros((depth, n_seq, att_w, seq), F32)
    vp_all = jnp.zeros((depth, n_seq, att_w, seq), F32)
    for l in range(depth):
        w = _layer_weights(l, norm_mix, w_in, q_gain, k_gain, gm_gain, conv_w, w_proj_attn,
                           w_proj_gmlp, w_proj_conv, w_out, norm_mlp, w_up, w_down)
        wsp_p, bsp_p, wsp_s, bsp_s = _spatial_weights(w_spatial[l], b_spatial[l], dec_seq)

        q, kp_all, kb, vp_all, vb, u, gvn, xc, cbcy = _in_proj(xp, w, None, tm_prompt, n_seq, K_TILE,
                                                               (l, kp_all, vp_all))
        att = _sb_prompt(q, kb, vb, sb_bias[l], tq)
        xp = _mlp(_merge(xp, att, u, gvn, cbcy, w, wsp_p, bsp_p, tm_prompt, False), w, tm_prompt)
        outs["cp"].append(xc.reshape(n_seq, seq, conv_width)[:, seq - (CONV_K - 1):])

        q, kf, kb, vf, vb, u, gvn, xc, cbcy = _in_proj(
            xs, w, (state_conv[l][:, 0], state_conv[l][:, 1]), n_dec, dec_seq, page)
        q3 = q.reshape(dec_seq, n_dec, att_w).transpose(1, 0, 2)
        qx = jnp.where(own_head[:, :, None, :], q3[:, None], jnp.zeros((), BF16))
        qx = qx.reshape(n_dec, n_heads * dec_seq, att_w)
        new_rows = lambda a: jnp.pad(a.reshape(dec_seq, att_w, n_dec).transpose(2, 0, 1),
                                     ((0, 0), (0, BF16_SUBLANES - dec_seq), (0, 0)))
        bias_rows = jnp.repeat(sb_bias[l], dec_seq)[:, None]
        att = _sb_decode(qx, new_rows(kb), new_rows(vb), bias_rows, cache_kt, cache_vt, l, page_table)
        att = att[:, :dec_seq].transpose(1, 0, 2).reshape(m_s, att_w)
        xs = _mlp(_merge(xs, att, u, gvn, cbcy, w, wsp_s, bsp_s, m_s, True), w, m_s)
        outs["ks"].append(kf)
        outs["vs"].append(vf)
        outs["cs"].append(xc.reshape(dec_seq, n_dec, conv_width)[dec_seq - (CONV_K - 1):].transpose(1, 0, 2))
        outs["gs"].append(gvn.reshape(dec_seq, n_dec, -1).transpose(1, 0, 2))

    stack = lambda name: jnp.stack(outs[name])
    prompt_kv = lambda a: a.reshape(depth, n_seq, n_heads, head_dim, seq).transpose(0, 1, 4, 2, 3)
    sample_kv = lambda name: stack(name).reshape(depth, dec_seq, n_heads, head_dim, n_dec).transpose(0, 4, 1, 2, 3)
    return (xp.reshape(n_seq, seq, d), xs.reshape(dec_seq, n_dec, d).transpose(1, 0, 2),
            prompt_kv(kp_all), prompt_kv(vp_all), stack("cp"),
            sample_kv("ks"), sample_kv("vs"), stack("cs"), stack("gs"))
```

```python
import functools

import jax
import jax.numpy as jnp
from jax import lax
from jax.experimental import pallas as pl
from jax.experimental.pallas import tpu as pltpu

HEAD_DIM = 64
GM_GROUP_DIM = 64
GM_GROUPS = 4
CONV_K = 3
CHUNK = 128
EPS = 1e-6
LOG2E = 1.4426950408889634
LN2 = 0.6931471805599453
LANES = 128
ROW_TILE = 512
Q_TILE = 512
K_TILE = 256
EARLY_P_NUM, EARLY_P_DEN = 3, 4
BF16_SUBLANES = 16
D_FF_CHUNK = 1024
MASKED_LOG2 = -1e30
BF16 = jnp.bfloat16
F32 = jnp.float32


def _dot(a, b):
    return jnp.dot(a, b, preferred_element_type=F32)


def _dot_nt(a, b):
    return lax.dot_general(a, b, (((1,), (1,)), ((), ())), preferred_element_type=F32)


def _rms_rows(x, gain):
    ms = jnp.mean(x * x, axis=-1, keepdims=True)
    return x * lax.rsqrt(ms + EPS) * gain


def _group_rms_lanes(y, ones_blockdiag, group):
    sq = y * y
    hi = sq.astype(BF16)
    lo = (sq - hi.astype(F32)).astype(BF16)
    ss = _dot(hi, ones_blockdiag) + _dot(lo, ones_blockdiag)
    return y * lax.rsqrt(ss * (1.0 / group) + EPS)


def _group_rms_rows(yt, group):
    f, n = yt.shape
    y3 = yt.reshape(f // group, group, n)
    ss = jnp.sum(y3 * y3, axis=1, keepdims=True)
    return (y3 * lax.rsqrt(ss * (1.0 / group) + EPS)).reshape(f, n)


def _tile_lanes(a, n):
    return a[:, :n] if n <= a.shape[1] else jnp.concatenate([a] * (n // a.shape[1]), axis=1)


def _in_proj_kernel(*refs, sample_mode, tiles_per_seq):
    if sample_mode:
        (x_ref, nm_ref, wq, wkt, wvt, wu, wgv, wcb, wcc, wcx, hs_att, hs_gm, qg, kgt, gmg, cw,
         buf0_ref, buf1_ref, q_o, kf_o, kb_o, vf_o, vb_o, u_o, gvn_o, xc_o, cbcy_o, prev1_ref, prev2_ref) = refs
    else:
        (x_ref, nm_ref, wq, wkt, wvt, wu, wgv, wcb, wcc, wcx, hs_att, hs_gm, qg, kgt, gmg, cw, _, _,
         q_o, kf_o, kb_o, vf_o, vb_o, u_o, gvn_o, xc_o, cbcy_o, tail_ref) = refs

    xn = _rms_rows(x_ref[...], nm_ref[...]).astype(BF16)
    tm = xn.shape[0]
    tks = kb_o.shape[2]

    q = _group_rms_lanes(_dot(xn, wq[...]), hs_att[...], HEAD_DIM) * (qg[...] * (LOG2E * HEAD_DIM ** -0.5))
    q_o[...] = q.astype(BF16)
    kt = _group_rms_rows(_dot_nt(wkt[...], xn), HEAD_DIM) * _tile_lanes(kgt[...], tm)
    vt = _dot_nt(wvt[...], xn)
    kf_o[...] = kt
    vf_o[...] = vt
    for j in range(tm // tks):
        kb_o[j] = kt[:, j * tks:(j + 1) * tks].astype(BF16)
        vb_o[j] = vt[:, j * tks:(j + 1) * tks].astype(BF16)
    u_o[...] = _dot(xn, wu[...])
    gvn_o[...] = _group_rms_lanes(_dot(xn, wgv[...]), hs_gm[...], GM_GROUP_DIM) * gmg[...]

    cb = _dot(xn, wcb[...])
    xc = _dot(xn, wcc[...]) * _dot(xn, wcx[...])
    xc_o[...] = xc
    if sample_mode:
        @pl.when(pl.program_id(0) == 0)
        def _():
            prev2_ref[...] = buf0_ref[...]
            prev1_ref[...] = buf1_ref[...]
        prev1 = prev1_ref[...]
        prev2 = prev2_ref[...]
        prev2_ref[...] = prev1
        prev1_ref[...] = xc
    else:
        @pl.when(pl.program_id(0) % tiles_per_seq == 0)
        def _():
            tail_ref[...] = jnp.zeros_like(tail_ref)
        row = lax.broadcasted_iota(jnp.int32, xc.shape, 0)
        last2 = jnp.broadcast_to(tail_ref[0:1, :], xc.shape)
        last1 = jnp.broadcast_to(tail_ref[1:2, :], xc.shape)
        prev1 = jnp.where(row == 0, last1, pltpu.roll(xc, 1, axis=0))
        prev2 = jnp.where(row == 0, last2, jnp.where(row == 1, last1, pltpu.roll(xc, 2, axis=0)))
        tail_ref[0:2, :] = xc[tm - 2:tm, :]
    cy = prev2 * cw[0:1, :] + prev1 * cw[1:2, :] + xc * cw[2:3, :]
    cbcy_o[...] = (cb * cy).astype(BF16)


def _in_proj(x, w, conv_state, tm, n_groups, tk_store, kv_all=None):
    m, d = x.shape
    sample_mode = conv_state is not None
    assert sample_mode == (kv_all is None)
    att_w = w["wq"].shape[1]
    gm_w = w["wu"].shape[1]
    group_len = m // n_groups
    tps = group_len // tm
    tks = min(tk_store, tm)
    full = lambda a: pl.BlockSpec(a.shape, lambda i: (0,) * a.ndim)
    rows = lambda width: pl.BlockSpec((tm, width), lambda i: (i, 0))
    ins = [x, w["norm_mix"], w["wq"], w["wkt"], w["wvt"], w["wu"], w["wgv"], w["wcb"], w["wcc"], w["wcx"],
           w["hs_att"], w["hs_gm"], w["q_gain"], w["k_gain_t"], w["gm_gain"], w["conv_w"]]
    in_specs = [rows(d)] + [full(a) for a in ins[1:]]
    if sample_mode:
        ins += list(conv_state)
        in_specs += [full(conv_state[0]), full(conv_state[1])]
        aliases = {}
        ft_f32 = jax.ShapeDtypeStruct((n_groups, att_w, group_len), F32)
        ft_f32_spec = pl.BlockSpec((None, att_w, tm), lambda i: (i // tps, 0, i % tps))
    else:
        layer, k_all, v_all = kv_all
        aliases = {len(ins): 1, len(ins) + 1: 3}
        ins += [k_all, v_all]
        in_specs += [pl.BlockSpec(memory_space=pl.ANY)] * 2
        ft_f32 = jax.ShapeDtypeStruct(k_all.shape, F32)
        ft_f32_spec = pl.BlockSpec((None, None, att_w, tm), lambda i: (layer, i // tps, 0, i % tps))
    ft_bf16 = jax.ShapeDtypeStruct((n_groups, group_len // tks, att_w, tks), BF16)
    ft_bf16_spec = pl.BlockSpec((None, tm // tks, att_w, tks), lambda i: (i // tps, i % tps, 0, 0))
    out_shape = [jax.ShapeDtypeStruct((m, att_w), BF16),
                 ft_f32, ft_bf16,
                 ft_f32, ft_bf16,
                 jax.ShapeDtypeStruct((m, gm_w), F32),
                 jax.ShapeDtypeStruct((m, gm_w), F32),
                 jax.ShapeDtypeStruct((m, gm_w), F32),
                 jax.ShapeDtypeStruct((m, gm_w), BF16)]
    out_specs = [rows(att_w), ft_f32_spec, ft_bf16_spec, ft_f32_spec, ft_bf16_spec,
                 rows(gm_w), rows(gm_w), rows(gm_w), rows(gm_w)]
    if sample_mode:
        scratch = [pltpu.VMEM((tm, gm_w), F32), pltpu.VMEM((tm, gm_w), F32)]
    else:
        scratch = [pltpu.VMEM((8, gm_w), F32)]
    return pl.pallas_call(
        functools.partial(_in_proj_kernel, sample_mode=sample_mode, tiles_per_seq=tps),
        grid=(m // tm,), in_specs=in_specs, out_specs=out_specs, out_shape=out_shape,
        scratch_shapes=scratch, input_output_aliases=aliases,
        compiler_params=pltpu.CompilerParams(dimension_semantics=("arbitrary",)),
        name="in_proj_sample" if sample_mode else "in_proj_prompt",
    )(*ins)


def _neg_log2_rest(z):
    return jnp.maximum(z, 0.0) + jnp.log(1.0 + jnp.exp2(-jnp.abs(z))) * (1.0 / LN2)


def _neg_suffix_ones(n):
    j = lax.broadcasted_iota(jnp.int32, (n, n), 0)
    s = lax.broadcasted_iota(jnp.int32, (n, n), 1)
    return jnp.where(j >= s, -1.0, 0.0).astype(BF16)


def _sb_weights(z, p_bf16, neg_suffix, carry):
    cs = _dot(p_bf16, neg_suffix)
    return jnp.exp2(z + cs + carry).astype(BF16), carry + cs[:, 0:1]


def _sb_prompt_kernel(bias_ref, q_ref, k_ref, v_ref, o_ref, acc_ref, carry_ref, z_ref, p_ref, a_ref,
                      *, tq, tk, early_rows):
    hp = pl.program_id(1)
    qb = pl.program_id(2)
    ratio = tq // tk
    n_stages = 4
    steady_start = max(ratio, n_stages - 1)
    assert 2 * ratio >= steady_start
    n_blocks = (qb + 1) * ratio
    q2 = q_ref[...]
    first = lax.broadcasted_iota(jnp.int32, (tq, LANES), 1) < HEAD_DIM
    qh = (jnp.where(first, q2, jnp.zeros_like(q2)), jnp.where(first, jnp.zeros_like(q2), q2))
    b2 = (bias_ref[2 * hp] * LOG2E, bias_ref[2 * hp + 1] * LOG2E)
    neg_suffix = _neg_suffix_ones(tk)
    acc_ref[...] = jnp.zeros_like(acc_ref)
    carry_ref[...] = jnp.zeros_like(carry_ref)

    def live_rows(t):
        return slice((ratio - 1 - t) * tk if isinstance(t, int) and t < ratio else 0, tq)

    def stage0(t):
        kblk = k_ref[n_blocks - 1 - t]
        rows = live_rows(t)
        for hh in range(2):
            z = _dot(qh[hh][rows], kblk) + b2[hh]
            if isinstance(t, int) and t < ratio:
                trow = lax.broadcasted_iota(jnp.int32, z.shape, 0)
                scol = lax.broadcasted_iota(jnp.int32, z.shape, 1)
                z = jnp.where(scol < trow, z, MASKED_LOG2)
            z_ref[t & 3, hh, rows] = z
            if rows.start < early_rows:
                p_ref[t & 1, hh, rows.start:early_rows] = _neg_log2_rest(z[:early_rows - rows.start]).astype(BF16)

    def stage1(t):
        rows = live_rows(t)
        rows = slice(max(rows.start, early_rows), tq)
        for hh in range(2):
            p_ref[t & 1, hh, rows] = _neg_log2_rest(z_ref[t & 3, hh, rows]).astype(BF16)

    def stage2(t):
        rows = live_rows(t)
        for hh in range(2):
            a_ref[t & 1, hh, rows], carry_ref[hh, rows] = _sb_weights(
                z_ref[t & 3, hh, rows], p_ref[t & 1, hh, rows], neg_suffix, carry_ref[hh, rows])
            if rows.start:
                a_ref[t & 1, hh, 0:rows.start] = jnp.zeros((rows.start, tk), BF16)

    def stage3(t):
        vblk = v_ref[n_blocks - 1 - t]
        pv = [_dot_nt(a_ref[t & 1, hh], vblk) for hh in range(2)]
        acc_ref[...] += jnp.where(first, pv[0], pv[1])

    stages = (stage0, stage1, stage2, stage3)

    def iteration(t, first_stage=0, last_stage=n_stages - 1):
        for s in reversed(range(first_stage, last_stage + 1)):
            stages[s](t - s)

    @pl.when(qb == 0)
    def _():
        for t in range(ratio):
            for stage in stages:
                stage(t)

    @pl.when(qb > 0)
    def _():
        for t in range(steady_start):
            iteration(t, last_stage=min(t, n_stages - 1))

        def body(t, c):
            iteration(t)
            return c
        lax.fori_loop(steady_start, n_blocks, body, 0)
        for i in range(1, n_stages):
            iteration(n_blocks - 1 + i, first_stage=i)

    o_ref[...] = acc_ref[...].astype(o_ref.dtype)


def _sb_prompt(q, kt, vt, bias, tq):
    m, att_w = q.shape
    n_seq, n_kblocks, _, tk = kt.shape
    seq = n_kblocks * tk
    n_pairs = att_w // LANES
    qblocks = seq // tq
    qspec = pl.BlockSpec((tq, LANES), lambda n, hp, qb: (n * qblocks + qb, hp))
    kvspec = pl.BlockSpec((None, n_kblocks, LANES, tk), lambda n, hp, qb: (n, 0, hp, 0))
    return pl.pallas_call(
        functools.partial(_sb_prompt_kernel, tq=tq, tk=tk, early_rows=tq * EARLY_P_NUM // EARLY_P_DEN),
        grid=(n_seq, n_pairs, qblocks),
        in_specs=[pl.BlockSpec(memory_space=pltpu.SMEM), qspec, kvspec, kvspec],
        out_specs=qspec,
        out_shape=jax.ShapeDtypeStruct((m, att_w), BF16),
        scratch_shapes=[pltpu.VMEM((tq, LANES), F32),
                        pltpu.VMEM((2, tq, 1), F32),
                        pltpu.VMEM((4, 2, tq, tk), F32),
                        pltpu.VMEM((2, 2, tq, tk), BF16),
                        pltpu.VMEM((2, 2, tq, tk), BF16)],
        compiler_params=pltpu.CompilerParams(dimension_semantics=("parallel", "parallel", "arbitrary")),
        name="sb_attention_prompt",
    )(bias, q, kt, vt)


def _sb_decode_kernel(pt_ref, bias_ref, qx_ref, kn_ref, vn_ref, *rest, n_pages):
    del pt_ref
    kpages = rest[:n_pages]
    vpages = rest[n_pages:2 * n_pages]
    o_ref = rest[2 * n_pages]
    qx = qx_ref[...]
    rows, att_w = qx.shape
    page = kpages[0].shape[1]
    dec_seq = o_ref.shape[0] // 2
    b2 = bias_ref[...] * LOG2E

    pad = jnp.zeros((page - kn_ref.shape[0], att_w), BF16)
    trow = lax.broadcasted_iota(jnp.int32, (rows, page), 0) & (dec_seq - 1)
    scol = lax.broadcasted_iota(jnp.int32, (rows, page), 1)
    z_new = jnp.where(scol < trow, _dot_nt(qx, jnp.concatenate([kn_ref[...], pad], axis=0)) + b2, MASKED_LOG2)
    z = jnp.concatenate([z_new] + [_dot(qx, kpages[i][...].astype(BF16)) + b2 for i in range(n_pages)], axis=0)
    cs = _dot(_neg_log2_rest(z).astype(BF16), _neg_suffix_ones(page))
    carries = [jnp.zeros((rows, 1), F32)]
    for i in range(n_pages):
        carries.append(carries[-1] + cs[i * rows:(i + 1) * rows, 0:1])
    a = jnp.exp2(z + cs + jnp.concatenate(carries, axis=0)).astype(BF16)
    acc = _dot(a[0:rows], jnp.concatenate([vn_ref[...], pad], axis=0))
    for i in range(n_pages):
        acc = acc + _dot_nt(a[(i + 1) * rows:(i + 2) * rows], vpages[i][...].astype(BF16))

    head_of_row = lax.broadcasted_iota(jnp.int32, acc.shape, 0) // dec_seq
    head_of_lane = lax.broadcasted_iota(jnp.int32, acc.shape, 1) // HEAD_DIM
    own = jnp.where(head_of_row == head_of_lane, acc, 0.0).astype(BF16)
    t_out = lax.broadcasted_iota(jnp.int32, (o_ref.shape[0], rows), 0)
    r_in = lax.broadcasted_iota(jnp.int32, (o_ref.shape[0], rows), 1)
    pick = jnp.where((r_in & (dec_seq - 1)) == t_out, 1.0, 0.0).astype(BF16)
    o_ref[...] = _dot(pick, own).astype(o_ref.dtype)


def _sb_decode(qx, k_new, v_new, bias_rows, cache_kt, cache_vt, layer, page_table):
    n_dec, rows, att_w = qx.shape
    new_rows = k_new.shape[1]
    page = cache_kt.shape[3]
    n_pages = page_table.shape[1]
    dec_seq = rows // (att_w // HEAD_DIM)

    def page_spec(i):
        return pl.BlockSpec((None, None, att_w, page), lambda b, pt: (layer, pt[b, n_pages - 1 - i], 0, 0))

    per_b = lambda r, c: pl.BlockSpec((None, r, c), lambda b, pt: (b, 0, 0))
    grid_spec = pltpu.PrefetchScalarGridSpec(
        num_scalar_prefetch=1,
        grid=(n_dec,),
        in_specs=([pl.BlockSpec(bias_rows.shape, lambda b, pt: (0, 0)), per_b(rows, att_w),
                   per_b(new_rows, att_w), per_b(new_rows, att_w)]
                  + [page_spec(i) for i in range(n_pages)] * 2),
        out_specs=per_b(2 * dec_seq, att_w),
    )
    return pl.pallas_call(
        functools.partial(_sb_decode_kernel, n_pages=n_pages),
        grid_spec=grid_spec,
        out_shape=jax.ShapeDtypeStruct((n_dec, 2 * dec_seq, att_w), BF16),
        compiler_params=pltpu.CompilerParams(dimension_semantics=("parallel",)),
        name="sb_attention_decode",
    )(page_table, bias_rows, qx, k_new, v_new, *([cache_kt] * n_pages), *([cache_vt] * n_pages))


def _merge_kernel(x_ref, nm_ref, wg_ref, att_ref, u_ref, gvn_ref, wsp_ref, bsp_ref, cbcy_ref,
                  wpa_ref, wpg_ref, wpc_ref, wout_ref, nm2_ref, wup_ref, wdn_ref, o_ref, *, sample_mode):
    x = x_ref[...]
    d = x.shape[1]
    xn = _rms_rows(x, nm_ref[...]).astype(BF16)
    tm = x.shape[0]
    gm_w = u_ref.shape[1]
    ug = []
    if sample_mode:
        n_tok = bsp_ref.shape[0]
        n_dec = tm // n_tok
        for t in range(n_tok):
            mixed = jnp.broadcast_to(bsp_ref[t:t + 1, :], (n_dec, gm_w))
            for s in range(t + 1):
                mixed = mixed + wsp_ref[t * n_tok + s:t * n_tok + s + 1, :] * gvn_ref[s * n_dec:(s + 1) * n_dec, :]
            ug.append((u_ref[t * n_dec:(t + 1) * n_dec, :] * mixed).astype(BF16))
    else:
        group_of_lane = lax.broadcasted_iota(jnp.int32, (CHUNK, gm_w), 1) // GM_GROUP_DIM
        for c in range(tm // CHUNK):
            sl = slice(c * CHUNK, (c + 1) * CHUNK)
            gv = gvn_ref[sl, :].astype(BF16)
            mixed = bsp_ref[...]
            for g in range(GM_GROUPS):
                mixed = mixed + jnp.where(group_of_lane == g, _dot(wsp_ref[g], gv), 0.0)
            ug.append((u_ref[sl, :] * mixed).astype(BF16))
    ug = jnp.concatenate(ug, axis=0)

    merged = jax.nn.sigmoid(_dot(xn, wg_ref[:, 0:d])) * _dot(att_ref[...], wpa_ref[...])
    merged = merged + jax.nn.sigmoid(_dot(xn, wg_ref[:, d:2 * d])) * _dot(ug, wpg_ref[...])
    merged = merged + jax.nn.sigmoid(_dot(xn, wg_ref[:, 2 * d:3 * d])) * _dot(cbcy_ref[...], wpc_ref[...])
    x = x + _dot(merged.astype(BF16), wout_ref[...])

    xn = _rms_rows(x, nm2_ref[...]).astype(BF16)
    acc = x
    for c in range(wup_ref.shape[1] // D_FF_CHUNK):
        sl = slice(c * D_FF_CHUNK, (c + 1) * D_FF_CHUNK)
        h = jnp.square(jnp.maximum(_dot(xn, wup_ref[:, sl]), 0.0)).astype(BF16)
        acc = acc + _dot(h, wdn_ref[sl, :])
    o_ref[...] = acc


def _merge_mlp(x, att, u, gvn, cbcy, w, wsp, bsp, tm, sample_mode):
    m, d = x.shape
    full = lambda a: pl.BlockSpec(a.shape, lambda i: (0,) * a.ndim, pipeline_mode=pl.Buffered(1))
    rows = lambda a: pl.BlockSpec((tm, a.shape[1]), lambda i: (i, 0))
    ins = [x, w["norm_mix"], w["wg"], att, u, gvn, wsp, bsp, cbcy, w["wpa"], w["wpg"], w["wpc"], w["wout"],
           w["norm_mlp"], w["wup"], w["wdn"]]
    is_rows = [True, False, False, True, True, True, False, False, True] + [False] * 7
    return pl.pallas_call(
        functools.partial(_merge_kernel, sample_mode=sample_mode), grid=(m // tm,),
        in_specs=[rows(a) if r else full(a) for a, r in zip(ins, is_rows)],
        out_specs=pl.BlockSpec((tm, d), lambda i: (i, 0)),
        out_shape=jax.ShapeDtypeStruct((m, d), F32),
        compiler_params=pltpu.CompilerParams(dimension_semantics=("parallel",)),
        name="merge_mlp_sample" if sample_mode else "merge_mlp_prompt",
    )(*ins)


def _blockdiag_ones(width, group):
    g = jnp.arange(width, dtype=jnp.int32) // group
    return (g[:, None] == g[None, :]).astype(BF16)


def _layer_weights(l, norm_mix, w_in, q_gain, k_gain, gm_gain, conv_w, w_proj_attn, w_proj_gmlp,
                   w_proj_conv, w_out, norm_mlp, w_up, w_down):
    d = w_in.shape[1]
    att_w = w_proj_attn.shape[1]
    gm_w = w_proj_gmlp.shape[1]
    conv_width = w_proj_conv.shape[1]
    n_heads = att_w // HEAD_DIM
    widths = [att_w] * 3 + [gm_w] * 2 + [conv_width] * 3 + [3 * d]
    offs = [0]
    for wd in widths:
        offs.append(offs[-1] + wd)
    wl = w_in[l].astype(BF16)
    names = ["wq", "wk", "wv", "wu", "wgv", "wcb", "wcc", "wcx", "wg"]
    w = {n: wl[:, offs[i]:offs[i + 1]] for i, n in enumerate(names)}
    w.update(
        wkt=w.pop("wk").T, wvt=w.pop("wv").T,
        norm_mix=norm_mix[l][None, :], norm_mlp=norm_mlp[l][None, :],
        q_gain=jnp.tile(q_gain[l], n_heads)[None, :],
        k_gain_t=jnp.broadcast_to(jnp.tile(k_gain[l], n_heads)[:, None], (att_w, LANES)),
        gm_gain=gm_gain[l][None, :], conv_w=conv_w[l],
        hs_att=_blockdiag_ones(att_w, HEAD_DIM), hs_gm=_blockdiag_ones(gm_w, GM_GROUP_DIM),
        wpa=w_proj_attn[l].astype(BF16), wpg=w_proj_gmlp[l].astype(BF16), wpc=w_proj_conv[l].astype(BF16),
        wout=w_out[l].astype(BF16), wup=w_up[l].astype(BF16), wdn=w_down[l].astype(BF16))
    return w


def _spatial_weights(w_spatial_l, b_spatial_l, dec_seq):
    r = jnp.arange(CHUNK, dtype=jnp.int32)
    tril = r[:, None] >= r[None, :]
    wsp_prompt = jnp.where(tril[None], w_spatial_l, 0.0).astype(BF16)
    bsp_prompt = jnp.repeat(b_spatial_l.T, GM_GROUP_DIM, axis=1)
    corner = w_spatial_l[:, :dec_seq, :dec_seq].reshape(GM_GROUPS, dec_seq * dec_seq)
    wsp_sample = jnp.repeat(corner.T, GM_GROUP_DIM, axis=1)
    bsp_sample = jnp.repeat(b_spatial_l[:, :dec_seq].T, GM_GROUP_DIM, axis=1)
    return wsp_prompt, bsp_prompt, wsp_sample, bsp_sample


def kernel(x_prompt, x_sample, cache_k, cache_v, state_conv, page_table, norm_mix, w_in, q_gain, k_gain, sb_bias, gm_gain, w_spatial, b_spatial, conv_w, w_proj_attn, w_proj_gmlp, w_proj_conv, w_out, norm_mlp, w_up, w_down):
    n_seq, seq, d = x_prompt.shape
    n_dec, dec_seq, _ = x_sample.shape
    depth, n_pool, page, n_heads, head_dim = cache_k.shape
    assert head_dim == HEAD_DIM and page == CHUNK and dec_seq == 4 and n_dec % 8 == 0
    att_w = n_heads * head_dim
    conv_width = state_conv.shape[-1]
    n_pages = page_table.shape[1]
    tm_prompt = min(ROW_TILE, seq)
    tq = min(Q_TILE, seq)
    assert seq % tm_prompt == 0 and seq % tq == 0 and tq % K_TILE == 0
    m_s = n_dec * dec_seq

    cache_kt = cache_k.transpose(0, 1, 3, 4, 2).reshape(depth, n_pool, att_w, page)
    cache_vt = cache_v.transpose(0, 1, 3, 4, 2).reshape(depth, n_pool, att_w, page)
    xp = x_prompt.reshape(n_seq * seq, d)
    xs = x_sample.transpose(1, 0, 2).reshape(m_s, d)
    head_of_lane = jnp.arange(att_w, dtype=jnp.int32) // HEAD_DIM
    own_head = head_of_lane[None, None, :] == jnp.arange(n_heads, dtype=jnp.int32)[None, :, None]

    outs = {name: [] for name in ("cp", "ks", "vs", "cs", "gs")}
    kp_all = jnp.zeros((depth, n_seq, att_w, seq), F32)
    vp_all = jnp.zeros((depth, n_seq, att_w, seq), F32)
    for l in range(depth):
        w = _layer_weights(l, norm_mix, w_in, q_gain, k_gain, gm_gain, conv_w, w_proj_attn,
                           w_proj_gmlp, w_proj_conv, w_out, norm_mlp, w_up, w_down)
        wsp_p, bsp_p, wsp_s, bsp_s = _spatial_weights(w_spatial[l], b_spatial[l], dec_seq)

        q, kp_all, kb, vp_all, vb, u, gvn, xc, cbcy = _in_proj(xp, w, None, tm_prompt, n_seq, K_TILE,
                                                               (l, kp_all, vp_all))
        att = _sb_prompt(q, kb, vb, sb_bias[l], tq)
        xp = _merge_mlp(xp, att, u, gvn, cbcy, w, wsp_p, bsp_p, tm_prompt, False)
        outs["cp"].append(xc.reshape(n_seq, seq, conv_width)[:, seq - (CONV_K - 1):])

        q, kf, kb, vf, vb, u, gvn, xc, cbcy = _in_proj(
            xs, w, (state_conv[l][:, 0], state_conv[l][:, 1]), n_dec, dec_seq, page)
        q3 = q.reshape(dec_seq, n_dec, att_w).transpose(1, 0, 2)
        qx = jnp.where(own_head[:, :, None, :], q3[:, None], jnp.zeros((), BF16))
        qx = qx.reshape(n_dec, n_heads * dec_seq, att_w)
        new_rows = lambda a: jnp.pad(a.reshape(dec_seq, att_w, n_dec).transpose(2, 0, 1),
                                     ((0, 0), (0, BF16_SUBLANES - dec_seq), (0, 0)))
        bias_rows = jnp.repeat(sb_bias[l], dec_seq)[:, None]
        att = _sb_decode(qx, new_rows(kb), new_rows(vb), bias_rows, cache_kt, cache_vt, l, page_table)
        att = att[:, :dec_seq].transpose(1, 0, 2).reshape(m_s, att_w)
        xs = _merge_mlp(xs, att, u, gvn, cbcy, w, wsp_s, bsp_s, m_s, True)
        outs["ks"].append(kf)
        outs["vs"].append(vf)
        outs["cs"].append(xc.reshape(dec_seq, n_dec, conv_width)[dec_seq - (CONV_K - 1):].transpose(1, 0, 2))
        outs["gs"].append(gvn.reshape(dec_seq, n_dec, -1).transpose(1, 0, 2))

    stack = lambda name: jnp.stack(outs[name])
    prompt_kv = lambda a: a.reshape(depth, n_seq, n_heads, head_dim, seq).transpose(0, 1, 4, 2, 3)
    sample_kv = lambda name: stack(name).reshape(depth, dec_seq, n_heads, head_dim, n_dec).transpose(0, 4, 1, 2, 3)
    return (xp.reshape(n_seq, seq, d), xs.reshape(dec_seq, n_dec, d).transpose(1, 0, 2),
            prompt_kv(kp_all), prompt_kv(vp_all), stack("cp"),
            sample_kv("ks"), sample_kv("vs"), stack("cs"), stack("gs"))
```

```python
import functools

import jax
import jax.numpy as jnp
from jax import lax
from jax.experimental import pallas as pl
from jax.experimental.pallas import tpu as pltpu

HEAD_DIM = 64
GM_GROUP_DIM = 64
GM_GROUPS = 4
CONV_K = 3
CHUNK = 128
EPS = 1e-6
LOG2E = 1.4426950408889634
LN2 = 0.6931471805599453
LANES = 128
ROW_TILE = 512
IN_PROJ_TILE = 1024
Q_TILE = 512
K_TILE = 256
EARLY_P_NUM, EARLY_P_DEN = 3, 4
BF16_SUBLANES = 16
D_FF_CHUNK = 1024
MASKED_LOG2 = -1e30
BF16 = jnp.bfloat16
F32 = jnp.float32


def _dot(a, b):
    return jnp.dot(a, b, preferred_element_type=F32)


def _dot_nt(a, b):
    return lax.dot_general(a, b, (((1,), (1,)), ((), ())), preferred_element_type=F32)


def _rms_rows(x, gain):
    ms = jnp.mean(x * x, axis=-1, keepdims=True)
    return x * lax.rsqrt(ms + EPS) * gain


def _group_rms_lanes(y, ones_blockdiag, group):
    sq = y * y
    hi = sq.astype(BF16)
    lo = (sq - hi.astype(F32)).astype(BF16)
    ss = _dot(hi, ones_blockdiag) + _dot(lo, ones_blockdiag)
    return y * lax.rsqrt(ss * (1.0 / group) + EPS)


def _group_rms_rows(yt, group):
    f, n = yt.shape
    y3 = yt.reshape(f // group, group, n)
    ss = jnp.sum(y3 * y3, axis=1, keepdims=True)
    return (y3 * lax.rsqrt(ss * (1.0 / group) + EPS)).reshape(f, n)


def _tile_lanes(a, n):
    return a[:, :n] if n <= a.shape[1] else jnp.concatenate([a] * (n // a.shape[1]), axis=1)


def _in_proj_kernel(*refs, sample_mode, tiles_per_seq):
    if sample_mode:
        (x_ref, nm_ref, wq, wkt, wvt, wu, wgv, wcb, wcc, wcx, hs_att, hs_gm, qg, kgt, gmg, cw,
         buf0_ref, buf1_ref, q_o, kf_o, kb_o, vf_o, vb_o, u_o, gvn_o, xc_o, cbcy_o, prev1_ref, prev2_ref) = refs
    else:
        (x_ref, nm_ref, wq, wkt, wvt, wu, wgv, wcb, wcc, wcx, hs_att, hs_gm, qg, kgt, gmg, cw, _, _,
         q_o, kf_o, kb_o, vf_o, vb_o, u_o, gvn_o, xc_o, cbcy_o, tail_ref) = refs

    xn = _rms_rows(x_ref[...], nm_ref[...]).astype(BF16)
    tm = xn.shape[0]
    tks = kb_o.shape[2]

    q = _group_rms_lanes(_dot(xn, wq[...]), hs_att[...], HEAD_DIM) * (qg[...] * (LOG2E * HEAD_DIM ** -0.5))
    q_o[...] = q.astype(BF16)
    kt = _group_rms_rows(_dot_nt(wkt[...], xn), HEAD_DIM) * _tile_lanes(kgt[...], tm)
    vt = _dot_nt(wvt[...], xn)
    kf_o[...] = kt
    vf_o[...] = vt
    for j in range(tm // tks):
        kb_o[j] = kt[:, j * tks:(j + 1) * tks].astype(BF16)
        vb_o[j] = vt[:, j * tks:(j + 1) * tks].astype(BF16)
    u_o[...] = _dot(xn, wu[...])
    gvn_o[...] = _group_rms_lanes(_dot(xn, wgv[...]), hs_gm[...], GM_GROUP_DIM) * gmg[...]

    cb = _dot(xn, wcb[...])
    xc = _dot(xn, wcc[...]) * _dot(xn, wcx[...])
    xc_o[...] = xc
    if sample_mode:
        @pl.when(pl.program_id(0) == 0)
        def _():
            prev2_ref[...] = buf0_ref[...]
            prev1_ref[...] = buf1_ref[...]
        prev1 = prev1_ref[...]
        prev2 = prev2_ref[...]
        prev2_ref[...] = prev1
        prev1_ref[...] = xc
    else:
        @pl.when(pl.program_id(0) % tiles_per_seq == 0)
        def _():
            tail_ref[...] = jnp.zeros_like(tail_ref)
        row = lax.broadcasted_iota(jnp.int32, xc.shape, 0)
        last2 = jnp.broadcast_to(tail_ref[0:1, :], xc.shape)
        last1 = jnp.broadcast_to(tail_ref[1:2, :], xc.shape)
        prev1 = jnp.where(row == 0, last1, pltpu.roll(xc, 1, axis=0))
        prev2 = jnp.where(row == 0, last2, jnp.where(row == 1, last1, pltpu.roll(xc, 2, axis=0)))
        tail_ref[0:2, :] = xc[tm - 2:tm, :]
    cy = prev2 * cw[0:1, :] + prev1 * cw[1:2, :] + xc * cw[2:3, :]
    cbcy_o[...] = (cb * cy).astype(BF16)


def _in_proj(x, w, conv_state, tm, n_groups, tk_store, kv_all=None):
    m, d = x.shape
    sample_mode = conv_state is not None
    assert sample_mode == (kv_all is None)
    att_w = w["wq"].shape[1]
    gm_w = w["wu"].shape[1]
    group_len = m // n_groups
    tps = group_len // tm
    tks = min(tk_store, tm)
    full = lambda a: pl.BlockSpec(a.shape, lambda i: (0,) * a.ndim, pipeline_mode=pl.Buffered(1))
    rows = lambda width: pl.BlockSpec((tm, width), lambda i: (i, 0))
    ins = [x, w["norm_mix"], w["wq"], w["wkt"], w["wvt"], w["wu"], w["wgv"], w["wcb"], w["wcc"], w["wcx"],
           w["hs_att"], w["hs_gm"], w["q_gain"], w["k_gain_t"], w["gm_gain"], w["conv_w"]]
    in_specs = [rows(d)] + [full(a) for a in ins[1:]]
    if sample_mode:
        ins += list(conv_state)
        in_specs += [full(conv_state[0]), full(conv_state[1])]
        aliases = {}
        ft_f32 = jax.ShapeDtypeStruct((n_groups, att_w, group_len), F32)
        ft_f32_spec = pl.BlockSpec((None, att_w, tm), lambda i: (i // tps, 0, i % tps))
    else:
        layer, k_all, v_all = kv_all
        aliases = {len(ins): 1, len(ins) + 1: 3}
        ins += [k_all, v_all]
        in_specs += [pl.BlockSpec(memory_space=pl.ANY)] * 2
        ft_f32 = jax.ShapeDtypeStruct(k_all.shape, F32)
        ft_f32_spec = pl.BlockSpec((None, None, att_w, tm), lambda i: (layer, i // tps, 0, i % tps))
    ft_bf16 = jax.ShapeDtypeStruct((n_groups, group_len // tks, att_w, tks), BF16)
    ft_bf16_spec = pl.BlockSpec((None, tm // tks, att_w, tks), lambda i: (i // tps, i % tps, 0, 0))
    out_shape = [jax.ShapeDtypeStruct((m, att_w), BF16),
                 ft_f32, ft_bf16,
                 ft_f32, ft_bf16,
                 jax.ShapeDtypeStruct((m, gm_w), F32),
                 jax.ShapeDtypeStruct((m, gm_w), F32),
                 jax.ShapeDtypeStruct((m, gm_w), F32),
                 jax.ShapeDtypeStruct((m, gm_w), BF16)]
    out_specs = [rows(att_w), ft_f32_spec, ft_bf16_spec, ft_f32_spec, ft_bf16_spec,
                 rows(gm_w), rows(gm_w), rows(gm_w), rows(gm_w)]
    if sample_mode:
        scratch = [pltpu.VMEM((tm, gm_w), F32), pltpu.VMEM((tm, gm_w), F32)]
    else:
        scratch = [pltpu.VMEM((8, gm_w), F32)]
    return pl.pallas_call(
        functools.partial(_in_proj_kernel, sample_mode=sample_mode, tiles_per_seq=tps),
        grid=(m // tm,), in_specs=in_specs, out_specs=out_specs, out_shape=out_shape,
        scratch_shapes=scratch, input_output_aliases=aliases,
        compiler_params=pltpu.CompilerParams(dimension_semantics=("arbitrary",)),
        name="in_proj_sample" if sample_mode else "in_proj_prompt",
    )(*ins)


def _neg_log2_rest(z):
    return jnp.maximum(z, 0.0) + jnp.log(1.0 + jnp.exp2(-jnp.abs(z))) * (1.0 / LN2)


def _neg_suffix_ones(n):
    j = lax.broadcasted_iota(jnp.int32, (n, n), 0)
    s = lax.broadcasted_iota(jnp.int32, (n, n), 1)
    return jnp.where(j >= s, -1.0, 0.0).astype(BF16)


def _sb_weights(z, p_bf16, neg_suffix, carry):
    cs = _dot(p_bf16, neg_suffix)
    return jnp.exp2(z + cs + carry).astype(BF16), carry + cs[:, 0:1]


def _sb_prompt_kernel(bias_ref, q_ref, k_ref, v_ref, o_ref, acc_ref, carry_ref, z_ref, p_ref, a_ref,
                      *, tq, tk, early_rows):
    hp = pl.program_id(1)
    qb = pl.program_id(2)
    ratio = tq // tk
    n_stages = 4
    steady_start = max(ratio, n_stages - 1)
    assert 2 * ratio >= steady_start
    n_blocks = (qb + 1) * ratio
    q2 = q_ref[...]
    first = lax.broadcasted_iota(jnp.int32, (tq, LANES), 1) < HEAD_DIM
    qh = (jnp.where(first, q2, jnp.zeros_like(q2)), jnp.where(first, jnp.zeros_like(q2), q2))
    b2 = (bias_ref[2 * hp] * LOG2E, bias_ref[2 * hp + 1] * LOG2E)
    neg_suffix = _neg_suffix_ones(tk)
    acc_ref[...] = jnp.zeros_like(acc_ref)
    carry_ref[...] = jnp.zeros_like(carry_ref)

    def live_rows(t):
        return slice((ratio - 1 - t) * tk if isinstance(t, int) and t < ratio else 0, tq)

    def stage0(t):
        kblk = k_ref[n_blocks - 1 - t]
        rows = live_rows(t)
        for hh in range(2):
            z = _dot(qh[hh][rows], kblk) + b2[hh]
            if isinstance(t, int) and t < ratio:
                trow = lax.broadcasted_iota(jnp.int32, z.shape, 0)
                scol = lax.broadcasted_iota(jnp.int32, z.shape, 1)
                z = jnp.where(scol < trow, z, MASKED_LOG2)
            z_ref[t & 3, hh, rows] = z
            if rows.start < early_rows:
                p_ref[t & 1, hh, rows.start:early_rows] = _neg_log2_rest(z[:early_rows - rows.start]).astype(BF16)

    def stage1(t):
        rows = live_rows(t)
        rows = slice(max(rows.start, early_rows), tq)
        for hh in range(2):
            p_ref[t & 1, hh, rows] = _neg_log2_rest(z_ref[t & 3, hh, rows]).astype(BF16)

    def stage2(t):
        rows = live_rows(t)
        for hh in range(2):
            a_ref[t & 1, hh, rows], carry_ref[hh, rows] = _sb_weights(
                z_ref[t & 3, hh, rows], p_ref[t & 1, hh, rows], neg_suffix, carry_ref[hh, rows])
            if rows.start:
                a_ref[t & 1, hh, 0:rows.start] = jnp.zeros((rows.start, tk), BF16)

    def stage3(t):
        vblk = v_ref[n_blocks - 1 - t]
        pv = [_dot_nt(a_ref[t & 1, hh], vblk) for hh in range(2)]
        acc_ref[...] += jnp.where(first, pv[0], pv[1])

    stages = (stage0, stage1, stage2, stage3)

    def iteration(t, first_stage=0, last_stage=n_stages - 1):
        for s in reversed(range(first_stage, last_stage + 1)):
            stages[s](t - s)

    @pl.when(qb == 0)
    def _():
        for t in range(ratio):
            for stage in stages:
                stage(t)

    @pl.when(qb > 0)
    def _():
        for t in range(steady_start):
            iteration(t, last_stage=min(t, n_stages - 1))

        def body(t, c):
            iteration(t)
            return c
        lax.fori_loop(steady_start, n_blocks, body, 0)
        for i in range(1, n_stages):
            iteration(n_blocks - 1 + i, first_stage=i)

    o_ref[...] = acc_ref[...].astype(o_ref.dtype)


def _sb_prompt(q, kt, vt, bias, tq):
    m, att_w = q.shape
    n_seq, n_kblocks, _, tk = kt.shape
    seq = n_kblocks * tk
    n_pairs = att_w // LANES
    qblocks = seq // tq
    qspec = pl.BlockSpec((tq, LANES), lambda n, hp, qb: (n * qblocks + qb, hp))
    kvspec = pl.BlockSpec((None, n_kblocks, LANES, tk), lambda n, hp, qb: (n, 0, hp, 0))
    return pl.pallas_call(
        functools.partial(_sb_prompt_kernel, tq=tq, tk=tk, early_rows=tq * EARLY_P_NUM // EARLY_P_DEN),
        grid=(n_seq, n_pairs, qblocks),
        in_specs=[pl.BlockSpec(memory_space=pltpu.SMEM), qspec, kvspec, kvspec],
        out_specs=qspec,
        out_shape=jax.ShapeDtypeStruct((m, att_w), BF16),
        scratch_shapes=[pltpu.VMEM((tq, LANES), F32),
                        pltpu.VMEM((2, tq, 1), F32),
                        pltpu.VMEM((4, 2, tq, tk), F32),
                        pltpu.VMEM((2, 2, tq, tk), BF16),
                        pltpu.VMEM((2, 2, tq, tk), BF16)],
        compiler_params=pltpu.CompilerParams(dimension_semantics=("parallel", "parallel", "arbitrary")),
        name="sb_attention_prompt",
    )(bias, q, kt, vt)


def _sb_decode_kernel(pt_ref, bias_ref, qx_ref, kn_ref, vn_ref, *rest, n_pages):
    del pt_ref
    kpages = rest[:n_pages]
    vpages = rest[n_pages:2 * n_pages]
    o_ref = rest[2 * n_pages]
    qx = qx_ref[...]
    rows, att_w = qx.shape
    page = kpages[0].shape[1]
    dec_seq = o_ref.shape[0] // 2
    b2 = bias_ref[...] * LOG2E

    pad = jnp.zeros((page - kn_ref.shape[0], att_w), BF16)
    trow = lax.broadcasted_iota(jnp.int32, (rows, page), 0) & (dec_seq - 1)
    scol = lax.broadcasted_iota(jnp.int32, (rows, page), 1)
    z_new = jnp.where(scol < trow, _dot_nt(qx, jnp.concatenate([kn_ref[...], pad], axis=0)) + b2, MASKED_LOG2)
    z = jnp.concatenate([z_new] + [_dot(qx, kpages[i][...].astype(BF16)) + b2 for i in range(n_pages)], axis=0)
    cs = _dot(_neg_log2_rest(z).astype(BF16), _neg_suffix_ones(page))
    carries = [jnp.zeros((rows, 1), F32)]
    for i in range(n_pages):
        carries.append(carries[-1] + cs[i * rows:(i + 1) * rows, 0:1])
    a = jnp.exp2(z + cs + jnp.concatenate(carries, axis=0)).astype(BF16)
    acc = _dot(a[0:rows], jnp.concatenate([vn_ref[...], pad], axis=0))
    for i in range(n_pages):
        acc = acc + _dot_nt(a[(i + 1) * rows:(i + 2) * rows], vpages[i][...].astype(BF16))

    head_of_row = lax.broadcasted_iota(jnp.int32, acc.shape, 0) // dec_seq
    head_of_lane = lax.broadcasted_iota(jnp.int32, acc.shape, 1) // HEAD_DIM
    own = jnp.where(head_of_row == head_of_lane, acc, 0.0).astype(BF16)
    t_out = lax.broadcasted_iota(jnp.int32, (o_ref.shape[0], rows), 0)
    r_in = lax.broadcasted_iota(jnp.int32, (o_ref.shape[0], rows), 1)
    pick = jnp.where((r_in & (dec_seq - 1)) == t_out, 1.0, 0.0).astype(BF16)
    o_ref[...] = _dot(pick, own).astype(o_ref.dtype)


def _sb_decode(qx, k_new, v_new, bias_rows, cache_kt, cache_vt, layer, page_table):
    n_dec, rows, att_w = qx.shape
    new_rows = k_new.shape[1]
    page = cache_kt.shape[3]
    n_pages = page_table.shape[1]
    dec_seq = rows // (att_w // HEAD_DIM)

    def page_spec(i):
        return pl.BlockSpec((None, None, att_w, page), lambda b, pt: (layer, pt[b, n_pages - 1 - i], 0, 0))

    per_b = lambda r, c: pl.BlockSpec((None, r, c), lambda b, pt: (b, 0, 0))
    grid_spec = pltpu.PrefetchScalarGridSpec(
        num_scalar_prefetch=1,
        grid=(n_dec,),
        in_specs=([pl.BlockSpec(bias_rows.shape, lambda b, pt: (0, 0)), per_b(rows, att_w),
                   per_b(new_rows, att_w), per_b(new_rows, att_w)]
                  + [page_spec(i) for i in range(n_pages)] * 2),
        out_specs=per_b(2 * dec_seq, att_w),
    )
    return pl.pallas_call(
        functools.partial(_sb_decode_kernel, n_pages=n_pages),
        grid_spec=grid_spec,
        out_shape=jax.ShapeDtypeStruct((n_dec, 2 * dec_seq, att_w), BF16),
        compiler_params=pltpu.CompilerParams(dimension_semantics=("parallel",)),
        name="sb_attention_decode",
    )(page_table, bias_rows, qx, k_new, v_new, *([cache_kt] * n_pages), *([cache_vt] * n_pages))


def _merge_kernel(x_ref, nm_ref, wg_ref, att_ref, u_ref, gvn_ref, wsp_ref, bsp_ref, cbcy_ref,
                  wpa_ref, wpg_ref, wpc_ref, wout_ref, nm2_ref, wup_ref, wdn_ref, o_ref, *, sample_mode):
    x = x_ref[...]
    d = x.shape[1]
    xn = _rms_rows(x, nm_ref[...]).astype(BF16)
    tm = x.shape[0]
    gm_w = u_ref.shape[1]
    ug = []
    if sample_mode:
        n_tok = bsp_ref.shape[0]
        n_dec = tm // n_tok
        for t in range(n_tok):
            mixed = jnp.broadcast_to(bsp_ref[t:t + 1, :], (n_dec, gm_w))
            for s in range(t + 1):
                mixed = mixed + wsp_ref[t * n_tok + s:t * n_tok + s + 1, :] * gvn_ref[s * n_dec:(s + 1) * n_dec, :]
            ug.append((u_ref[t * n_dec:(t + 1) * n_dec, :] * mixed).astype(BF16))
    else:
        group_of_lane = lax.broadcasted_iota(jnp.int32, (CHUNK, gm_w), 1) // GM_GROUP_DIM
        for c in range(tm // CHUNK):
            sl = slice(c * CHUNK, (c + 1) * CHUNK)
            gv = gvn_ref[sl, :].astype(BF16)
            mixed = bsp_ref[...]
            for g in range(GM_GROUPS):
                mixed = mixed + jnp.where(group_of_lane == g, _dot(wsp_ref[g], gv), 0.0)
            ug.append((u_ref[sl, :] * mixed).astype(BF16))
    ug = jnp.concatenate(ug, axis=0)

    merged = jax.nn.sigmoid(_dot(xn, wg_ref[:, 0:d])) * _dot(att_ref[...], wpa_ref[...])
    merged = merged + jax.nn.sigmoid(_dot(xn, wg_ref[:, d:2 * d])) * _dot(ug, wpg_ref[...])
    merged = merged + jax.nn.sigmoid(_dot(xn, wg_ref[:, 2 * d:3 * d])) * _dot(cbcy_ref[...], wpc_ref[...])
    x = x + _dot(merged.astype(BF16), wout_ref[...])

    xn = _rms_rows(x, nm2_ref[...]).astype(BF16)
    acc = x
    for c in range(wup_ref.shape[1] // D_FF_CHUNK):
        sl = slice(c * D_FF_CHUNK, (c + 1) * D_FF_CHUNK)
        h = jnp.square(jnp.maximum(_dot(xn, wup_ref[:, sl]), 0.0)).astype(BF16)
        acc = acc + _dot(h, wdn_ref[sl, :])
    o_ref[...] = acc


def _merge_mlp(x, att, u, gvn, cbcy, w, wsp, bsp, tm, sample_mode):
    m, d = x.shape
    full = lambda a: pl.BlockSpec(a.shape, lambda i: (0,) * a.ndim, pipeline_mode=pl.Buffered(1))
    rows = lambda a: pl.BlockSpec((tm, a.shape[1]), lambda i: (i, 0))
    ins = [x, w["norm_mix"], w["wg"], att, u, gvn, wsp, bsp, cbcy, w["wpa"], w["wpg"], w["wpc"], w["wout"],
           w["norm_mlp"], w["wup"], w["wdn"]]
    is_rows = [True, False, False, True, True, True, False, False, True] + [False] * 7
    return pl.pallas_call(
        functools.partial(_merge_kernel, sample_mode=sample_mode), grid=(m // tm,),
        in_specs=[rows(a) if r else full(a) for a, r in zip(ins, is_rows)],
        out_specs=pl.BlockSpec((tm, d), lambda i: (i, 0)),
        out_shape=jax.ShapeDtypeStruct((m, d), F32),
        compiler_params=pltpu.CompilerParams(dimension_semantics=("parallel",)),
        name="merge_mlp_sample" if sample_mode else "merge_mlp_prompt",
    )(*ins)


def _blockdiag_ones(width, group):
    g = jnp.arange(width, dtype=jnp.int32) // group
    return (g[:, None] == g[None, :]).astype(BF16)


def _layer_weights(l, norm_mix, w_in, q_gain, k_gain, gm_gain, conv_w, w_proj_attn, w_proj_gmlp,
                   w_proj_conv, w_out, norm_mlp, w_up, w_down):
    d = w_in.shape[1]
    att_w = w_proj_attn.shape[1]
    gm_w = w_proj_gmlp.shape[1]
    conv_width = w_proj_conv.shape[1]
    n_heads = att_w // HEAD_DIM
    widths = [att_w] * 3 + [gm_w] * 2 + [conv_width] * 3 + [3 * d]
    offs = [0]
    for wd in widths:
        offs.append(offs[-1] + wd)
    wl = w_in[l].astype(BF16)
    names = ["wq", "wk", "wv", "wu", "wgv", "wcb", "wcc", "wcx", "wg"]
    w = {n: wl[:, offs[i]:offs[i + 1]] for i, n in enumerate(names)}
    w.update(
        wkt=w.pop("wk").T, wvt=w.pop("wv").T,
        norm_mix=norm_mix[l][None, :], norm_mlp=norm_mlp[l][None, :],
        q_gain=jnp.tile(q_gain[l], n_heads)[None, :],
        k_gain_t=jnp.broadcast_to(jnp.tile(k_gain[l], n_heads)[:, None], (att_w, LANES)),
        gm_gain=gm_gain[l][None, :], conv_w=conv_w[l],
        hs_att=_blockdiag_ones(att_w, HEAD_DIM), hs_gm=_blockdiag_ones(gm_w, GM_GROUP_DIM),
        wpa=w_proj_attn[l].astype(BF16), wpg=w_proj_gmlp[l].astype(BF16), wpc=w_proj_conv[l].astype(BF16),
        wout=w_out[l].astype(BF16), wup=w_up[l].astype(BF16), wdn=w_down[l].astype(BF16))
    return w


def _spatial_weights(w_spatial_l, b_spatial_l, dec_seq):
    r = jnp.arange(CHUNK, dtype=jnp.int32)
    tril = r[:, None] >= r[None, :]
    wsp_prompt = jnp.where(tril[None], w_spatial_l, 0.0).astype(BF16)
    bsp_prompt = jnp.repeat(b_spatial_l.T, GM_GROUP_DIM, axis=1)
    corner = w_spatial_l[:, :dec_seq, :dec_seq].reshape(GM_GROUPS, dec_seq * dec_seq)
    wsp_sample = jnp.repeat(corner.T, GM_GROUP_DIM, axis=1)
    bsp_sample = jnp.repeat(b_spatial_l[:, :dec_seq].T, GM_GROUP_DIM, axis=1)
    return wsp_prompt, bsp_prompt, wsp_sample, bsp_sample


def kernel(x_prompt, x_sample, cache_k, cache_v, state_conv, page_table, norm_mix, w_in, q_gain, k_gain, sb_bias, gm_gain, w_spatial, b_spatial, conv_w, w_proj_attn, w_proj_gmlp, w_proj_conv, w_out, norm_mlp, w_up, w_down):
    n_seq, seq, d = x_prompt.shape
    n_dec, dec_seq, _ = x_sample.shape
    depth, n_pool, page, n_heads, head_dim = cache_k.shape
    assert head_dim == HEAD_DIM and page == CHUNK and dec_seq == 4 and n_dec % 8 == 0
    att_w = n_heads * head_dim
    conv_width = state_conv.shape[-1]
    n_pages = page_table.shape[1]
    tm_prompt = min(ROW_TILE, seq)
    tq = min(Q_TILE, seq)
    tm_in_proj = min(IN_PROJ_TILE, seq)
    assert seq % tm_prompt == 0 and seq % tm_in_proj == 0 and seq % tq == 0 and tq % K_TILE == 0
    m_s = n_dec * dec_seq

    cache_kt = cache_k.transpose(0, 1, 3, 4, 2).reshape(depth, n_pool, att_w, page)
    cache_vt = cache_v.transpose(0, 1, 3, 4, 2).reshape(depth, n_pool, att_w, page)
    xp = x_prompt.reshape(n_seq * seq, d)
    xs = x_sample.transpose(1, 0, 2).reshape(m_s, d)
    head_of_lane = jnp.arange(att_w, dtype=jnp.int32) // HEAD_DIM
    own_head = head_of_lane[None, None, :] == jnp.arange(n_heads, dtype=jnp.int32)[None, :, None]

    outs = {name: [] for name in ("cp", "ks", "vs", "cs", "gs")}
    kp_all = jnp.zeros((depth, n_seq, att_w, seq), F32)
    vp_all = jnp.zeros((depth, n_seq, att_w, seq), F32)
    for l in range(depth):
        w = _layer_weights(l, norm_mix, w_in, q_gain, k_gain, gm_gain, conv_w, w_proj_attn,
                           w_proj_gmlp, w_proj_conv, w_out, norm_mlp, w_up, w_down)
        wsp_p, bsp_p, wsp_s, bsp_s = _spatial_weights(w_spatial[l], b_spatial[l], dec_seq)

        q, kp_all, kb, vp_all, vb, u, gvn, xc, cbcy = _in_proj(xp, w, None, tm_in_proj, n_seq, K_TILE,
                                                               (l, kp_all, vp_all))
        att = _sb_prompt(q, kb, vb, sb_bias[l], tq)
        xp = _merge_mlp(xp, att, u, gvn, cbcy, w, wsp_p, bsp_p, tm_prompt, False)
        outs["cp"].append(xc.reshape(n_seq, seq, conv_width)[:, seq - (CONV_K - 1):])

        q, kf, kb, vf, vb, u, gvn, xc, cbcy = _in_proj(
            xs, w, (state_conv[l][:, 0], state_conv[l][:, 1]), n_dec, dec_seq, page)
        q3 = q.reshape(dec_seq, n_dec, att_w).transpose(1, 0, 2)
        qx = jnp.where(own_head[:, :, None, :], q3[:, None], jnp.zeros((), BF16))
        qx = qx.reshape(n_dec, n_heads * dec_seq, att_w)
        new_rows = lambda a: jnp.pad(a.reshape(dec_seq, att_w, n_dec).transpose(2, 0, 1),
                                     ((0, 0), (0, BF16_SUBLANES - dec_seq), (0, 0)))
        bias_rows = jnp.repeat(sb_bias[l], dec_seq)[:, None]
        att = _sb_decode(qx, new_rows(kb), new_rows(vb), bias_rows, cache_kt, cache_vt, l, page_table)
        att = att[:, :dec_seq].transpose(1, 0, 2).reshape(m_s, att_w)
        xs = _merge_mlp(xs, att, u, gvn, cbcy, w, wsp_s, bsp_s, m_s, True)
        outs["ks"].append(kf)
        outs["vs"].append(vf)
        outs["cs"].append(xc.reshape(dec_seq, n_dec, conv_width)[dec_seq - (CONV_K - 1):].transpose(1, 0, 2))
        outs["gs"].append(gvn.reshape(dec_seq, n_dec, -1).transpose(1, 0, 2))

    stack = lambda name: jnp.stack(outs[name])
    prompt_kv = lambda a: a.reshape(depth, n_seq, n_heads, head_dim, seq).transpose(0, 1, 4, 2, 3)
    sample_kv = lambda name: stack(name).reshape(depth, dec_seq, n_heads, head_dim, n_dec).transpose(0, 4, 1, 2, 3)
    return (xp.reshape(n_seq, seq, d), xs.reshape(dec_seq, n_dec, d).transpose(1, 0, 2),
            prompt_kv(kp_all), prompt_kv(vp_all), stack("cp"),
            sample_kv("ks"), sample_kv("vs"), stack("cs"), stack("gs"))
```
